```python
import math
import jax, jax.numpy as jnp
from jax import lax
import numpy as np

D_MODEL = 1024
BATCH = 16
SEQ = 4096
DEPTH = 1

HEAD_DIM = 64
SB_HEADS = 8
DIL_GROUPS = ((128, 1), (512, 4), (2048, 16))
DIL_HEADS_PER_GROUP = 4
DIL_HEADS = DIL_HEADS_PER_GROUP * len(DIL_GROUPS)
N_BRANCHES = 2
Q_BLOCK = 128
PEER_HEADS = 8
PEER_N_KEYS = 128
PEER_N_EXPERTS = PEER_N_KEYS * PEER_N_KEYS
PEER_QUERY_DIM = 256
PEER_TOPK = 16
TOKEN_CHUNK = 128
RMS_EPS = 1e-6
ALIBI_MAX_EXP = 8.0

SB_W = SB_HEADS * HEAD_DIM
DIL_W = DIL_HEADS * HEAD_DIM
DIL_OUT_W = DIL_HEADS_PER_GROUP * HEAD_DIM
IN_PROJ_W = 3 * SB_W + 3 * DIL_W + N_BRANCHES * D_MODEL

kernel_name = "hybrid_stickbreak_dilated_peer_block"


def rmsnorm(x, gain):
    xf = x.astype(jnp.float32)
    y = xf * lax.rsqrt(jnp.mean(xf * xf, axis=-1, keepdims=True) + RMS_EPS)
    return (y * gain.astype(jnp.float32)).astype(x.dtype)


def stick_breaking_attention(q, k, v):
    B, S, H, dh = q.shape
    nb = S // Q_BLOCK
    scale = dh ** -0.5
    q_blocks = q.reshape(B, nb, Q_BLOCK, H, dh).transpose(1, 0, 2, 3, 4)
    key_pos = jnp.arange(S)

    def block(args):
        qb, bi = args
        q_pos = bi * Q_BLOCK + jnp.arange(Q_BLOCK)
        z = jnp.einsum('bqhd,bshd->bhqs', qb, k).astype(jnp.float32) * scale
        causal = key_pos[None, :] < q_pos[:, None]
        log_beta = jax.nn.log_sigmoid(z)
        log_stay = jnp.where(causal, jax.nn.log_sigmoid(-z), 0.0)
        cum = jnp.cumsum(log_stay, axis=-1)
        log_a = log_beta + (cum[..., -1:] - cum)
        a = jnp.where(causal, jnp.exp(log_a), 0.0)
        return jnp.einsum('bhqs,bshd->bqhd', a.astype(v.dtype), v)

    out = lax.map(block, (q_blocks, jnp.arange(nb)))
    return out.transpose(1, 0, 2, 3, 4).reshape(B, S, H, dh)


def dilated_attention(q, k, v, slopes):
    B, S, H, dh = q.shape
    Hg = DIL_HEADS_PER_GROUP
    nb = S // Q_BLOCK
    scale = dh ** -0.5
    q_blocks = q.reshape(B, nb, Q_BLOCK, H, dh).transpose(1, 0, 2, 3, 4)
    k_groups = [k[:, :, g * Hg:(g + 1) * Hg] for g in range(len(DIL_GROUPS))]
    v_groups = [v[:, :, g * Hg:(g + 1) * Hg] for g in range(len(DIL_GROUPS))]

    def block(args):
        qb, bi = args
        q_pos = bi * Q_BLOCK + jnp.arange(Q_BLOCK)
        outs, lses = [], []
        for g, (window, dil) in enumerate(DIL_GROUPS):
            offs = dil * jnp.arange(window // dil + 1)
            idx = q_pos[:, None] - offs[None, :]
            valid = idx >= 0
            idx = jnp.maximum(idx, 0)
            kg = jnp.take(k_groups[g], idx, axis=1)
            vg = jnp.take(v_groups[g], idx, axis=1)
            qg = qb[:, :, g * Hg:(g + 1) * Hg]
            z = jnp.einsum('bqhd,bqjhd->bhqj', qg, kg).astype(jnp.float32) * scale
            z = z - slopes[g * Hg:(g + 1) * Hg][:, None, None] * offs.astype(jnp.float32)[None, None, :]
            z = jnp.where(valid[None, None], z, -1e30)
            m = jnp.max(z, axis=-1, keepdims=True)
            p = jnp.exp(z - m)
            den = jnp.sum(p, axis=-1, keepdims=True)
            o = jnp.einsum('bhqj,bqjhd->bqhd', (p / den).astype(vg.dtype), vg)
            outs.append(o)
            lses.append((m + jnp.log(den))[..., 0])
        w = jax.nn.softmax(jnp.stack(lses, axis=0), axis=0)
        w = w.transpose(0, 1, 3, 2)[..., None]
        o_all = jnp.stack(outs, axis=0)
        return jnp.sum(w.astype(o_all.dtype) * o_all, axis=0)

    out = lax.map(block, (q_blocks, jnp.arange(nb)))
    return out.transpose(1, 0, 2, 3, 4).reshape(B, S, Hg, dh)


def peer_ffn(h, w_q, sub_keys, expert_u, expert_v):
    B, S, D = h.shape
    t = h.reshape(-1, D)
    T = t.shape[0]
    chunks = t.reshape(T // TOKEN_CHUNK, TOKEN_CHUNK, D)
    K = PEER_TOPK

    def chunk(xc):
        qh = (xc @ w_q).reshape(TOKEN_CHUNK, PEER_HEADS, 2, PEER_QUERY_DIM // 2)
        s = jnp.einsum('thpc,hpnc->thpn', qh, sub_keys).astype(jnp.float32)
        s_top, i_top = lax.top_k(s, K)
        cand = (s_top[:, :, 0, :, None] + s_top[:, :, 1, None, :]).reshape(TOKEN_CHUNK, PEER_HEADS, K * K)
        best, ci = lax.top_k(cand, K)
        i1 = jnp.take_along_axis(i_top[:, :, 0], ci // K, axis=-1)
        i2 = jnp.take_along_axis(i_top[:, :, 1], ci % K, axis=-1)
        e = i1 * PEER_N_KEYS + i2
        gates = jax.nn.softmax(best, axis=-1)
        u = jnp.take(expert_u, e, axis=0)
        vv = jnp.take(expert_v, e, axis=0)
        act = jax.nn.gelu(jnp.einsum('thkd,td->thk', u, xc).astype(jnp.float32), approximate=False)
        return jnp.einsum('thk,thkd->td', (gates * act).astype(vv.dtype), vv)

    out = lax.map(chunk, chunks)
    return out.reshape(B, S, D)


def setup_inputs(seed: int = 0) -> dict:
    key = jax.random.key(seed)
    ks = jax.random.split(key, 14)
    D = D_MODEL
    f32 = jnp.float32
    nrm = lambda k, shape, s: jax.random.normal(k, shape, f32) * s
    return {
        "x": jax.random.normal(ks[0], (BATCH, SEQ, D), f32),
        "norm1_gain": 1.0 + nrm(ks[1], (DEPTH, D), 0.02),
        "w_in": nrm(ks[2], (DEPTH, D, IN_PROJ_W), D ** -0.5),
        "b_gate": nrm(ks[3], (DEPTH, N_BRANCHES * D), 0.02),
        "q_norm_gain": 1.0 + nrm(ks[4], (DEPTH, DIL_HEADS, HEAD_DIM), 0.02),
        "k_norm_gain": 1.0 + nrm(ks[5], (DEPTH, DIL_HEADS, HEAD_DIM), 0.02),
        "w_sb_out": nrm(ks[6], (DEPTH, SB_W, D), SB_W ** -0.5),
        "w_dil_out": nrm(ks[7], (DEPTH, DIL_OUT_W, D), DIL_OUT_W ** -0.5),
        "w_out": nrm(ks[8], (DEPTH, D, D), D ** -0.5),
        "norm2_gain": 1.0 + nrm(ks[9], (DEPTH, D), 0.02),
        "w_peer_q": nrm(ks[10], (DEPTH, D, PEER_HEADS * PEER_QUERY_DIM), D ** -0.5),
        "peer_sub_keys": nrm(ks[11], (DEPTH, PEER_HEADS, 2, PEER_N_KEYS, PEER_QUERY_DIM // 2), (PEER_QUERY_DIM // 2) ** -0.5),
        "peer_u": nrm(ks[12], (DEPTH, PEER_N_EXPERTS, D), D ** -0.5),
        "peer_v": nrm(ks[13], (DEPTH, PEER_N_EXPERTS, D), 0.3),
    }


def reference(x, norm1_gain, w_in, b_gate, q_norm_gain, k_norm_gain, w_sb_out, w_dil_out, w_out,
              norm2_gain, w_peer_q, peer_sub_keys, peer_u, peer_v):
    B, S, D = x.shape
    splits = [SB_W, 2 * SB_W, 3 * SB_W, 3 * SB_W + DIL_W, 3 * SB_W + 2 * DIL_W, 3 * SB_W + 3 * DIL_W]
    slopes = 2.0 ** (-ALIBI_MAX_EXP * jnp.arange(1, DIL_HEADS + 1, dtype=jnp.float32) / DIL_HEADS)
    for l in range(DEPTH):
        xn = rmsnorm(x, norm1_gain[l])
        proj = xn @ w_in[l]
        q_sb, k_sb, v_sb, q_dl, k_dl, v_dl, gate_logits = jnp.split(proj, splits, axis=-1)
        hd = lambda t, h: t.reshape(B, S, h, HEAD_DIM)
        o_sb = stick_breaking_attention(hd(q_sb, SB_HEADS), hd(k_sb, SB_HEADS), hd(v_sb, SB_HEADS))
        q_dl = rmsnorm(hd(q_dl, DIL_HEADS), q_norm_gain[l])
        k_dl = rmsnorm(hd(k_dl, DIL_HEADS), k_norm_gain[l])
        o_dl = dilated_attention(q_dl, k_dl, hd(v_dl, DIL_HEADS), slopes)
        y_sb = o_sb.reshape(B, S, SB_W) @ w_sb_out[l]
        y_dl = o_dl.reshape(B, S, DIL_OUT_W) @ w_dil_out[l]
        gates = jax.nn.sigmoid((gate_logits + b_gate[l]).astype(jnp.float32)).astype(x.dtype)
        g_sb, g_dl = jnp.split(gates, 2, axis=-1)
        x = x + (g_sb * y_sb + g_dl * y_dl) @ w_out[l]
        x = x + peer_ffn(rmsnorm(x, norm2_gain[l]), w_peer_q[l], peer_sub_keys[l], peer_u[l], peer_v[l])
    return x
```

```python
import functools
import math

import numpy as np
import jax
import jax.numpy as jnp
from jax import lax
from jax.experimental import pallas as pl
from jax.experimental.pallas import tpu as pltpu

F32 = jnp.float32
BF16 = jnp.bfloat16

HEAD_DIM = 64
SB_HEADS = 8
DIL_GROUPS = ((128, 1), (512, 4), (2048, 16))
DIL_HEADS_PER_GROUP = 4
DIL_HEADS = DIL_HEADS_PER_GROUP * len(DIL_GROUPS)
PEER_HEADS = 8
PEER_N_KEYS = 128
PEER_TOPK = 16
RMS_EPS = 1e-6
ALIBI_MAX_EXP = 8.0
SB_W = SB_HEADS * HEAD_DIM
DIL_W = DIL_HEADS * HEAD_DIM
DIL_GROUP_W = DIL_HEADS_PER_GROUP * HEAD_DIM
QK_SCALE = HEAD_DIM ** -0.5
NEG_BIG = -1e30

LANES = 128
DIL_BLOCK = 128
VMEM_LIMIT = 56 * 1024 * 1024


def _slopes():
    h = np.arange(1, DIL_HEADS + 1, dtype=np.float32)
    return np.float32(2.0) ** (np.float32(-ALIBI_MAX_EXP) * h / np.float32(DIL_HEADS))


def _params(sem):
    return pltpu.CompilerParams(dimension_semantics=sem, vmem_limit_bytes=VMEM_LIMIT)


def _const_spec(shape):
    return pl.BlockSpec(shape, lambda *_: (0,) * len(shape))


def _in_proj_kernel(x_ref, g1_ref, w_ref, bg_ref, gq_ref, gk_ref, ones_ref,
                    qsb_ref, ksb_ref, vsb_ref,
                    qd0_ref, qd1_ref, qd2_ref, kd0_ref, kd1_ref, kd2_ref,
                    vd0_ref, vd1_ref, vd2_ref, gate_ref, scr_ref, *, tm):
    x = x_ref[...]
    ms = jnp.mean(x * x, axis=-1, keepdims=True)
    xn = (x * lax.rsqrt(ms + RMS_EPS) * g1_ref[...]).astype(BF16)

    def proj(lo, width):
        return jnp.dot(xn, w_ref[:, lo:lo + width], preferred_element_type=F32)

    qsb_ref[...] = (proj(0, SB_W) * QK_SCALE).astype(BF16)
    ksb_ref[...] = proj(SB_W, SB_W).astype(BF16)
    vsb_ref[...] = proj(2 * SB_W, SB_W).astype(BF16)

    def head_norm(t, gain):
        ssq = jnp.dot((t * t).astype(BF16), ones_ref[...], preferred_element_type=F32)
        return t * lax.rsqrt(ssq * (1.0 / HEAD_DIM) + RMS_EPS) * gain

    def split_groups(t, outs):
        outs[0][0] = t[:, :DIL_GROUP_W].astype(BF16)
        for g in (1, 2):
            r = DIL_GROUPS[g][1]
            for hf in range(DIL_GROUP_W // LANES):
                lo = g * DIL_GROUP_W + hf * LANES
                scr_ref[hf] = t[:, lo:lo + LANES]
            for c in range(r):
                for hf in range(DIL_GROUP_W // LANES):
                    rows = scr_ref[hf, pl.ds(c, tm // r, stride=r), :]
                    outs[g][0, c, :, hf * LANES:(hf + 1) * LANES] = rows.astype(BF16)

    base = 3 * SB_W
    q = head_norm(proj(base, DIL_W), gq_ref[...]) * QK_SCALE
    split_groups(q, (qd0_ref, qd1_ref, qd2_ref))
    k = head_norm(proj(base + DIL_W, DIL_W), gk_ref[...])
    split_groups(k, (kd0_ref, kd1_ref, kd2_ref))
    split_groups(proj(base + 2 * DIL_W, DIL_W), (vd0_ref, vd1_ref, vd2_ref))

    gw = gate_ref.shape[-1]
    gate_ref[...] = jax.nn.sigmoid(proj(base + 3 * DIL_W, gw) + bg_ref[...]).astype(BF16)


def _in_proj(x2, g1, w_in, b_gate, gq, gk, ones_bd, *, batch, seq, tm):
    T, D = x2.shape
    tiles_per_seq = seq // tm
    gw = b_gate.shape[-1]
    row = lambda i: (i, 0)
    nat = pl.BlockSpec((1, tm, DIL_GROUP_W), lambda i: (i // tiles_per_seq, i % tiles_per_seq, 0))

    def perm_spec(r):
        return pl.BlockSpec((1, r, tm // r, DIL_GROUP_W),
                            lambda i: (i // tiles_per_seq, 0, i % tiles_per_seq, 0))

    def perm_shape(r):
        return jax.ShapeDtypeStruct((batch, r, seq // r, DIL_GROUP_W), BF16)

    r1, r2 = DIL_GROUPS[1][1], DIL_GROUPS[2][1]
    dil_shapes = [jax.ShapeDtypeStruct((batch, seq, DIL_GROUP_W), BF16), perm_shape(r1), perm_shape(r2)]
    dil_specs = [nat, perm_spec(r1), perm_spec(r2)]
    out_shape = ([jax.ShapeDtypeStruct((T, SB_W), BF16)] * 3 + dil_shapes * 3
                 + [jax.ShapeDtypeStruct((T, gw), BF16)])
    out_specs = ([pl.BlockSpec((tm, SB_W), row)] * 3 + dil_specs * 3 + [pl.BlockSpec((tm, gw), row)])
    return pl.pallas_call(
        functools.partial(_in_proj_kernel, tm=tm),
        grid=(T // tm,),
        in_specs=[pl.BlockSpec((tm, D), row), _const_spec(g1.shape), _const_spec(w_in.shape),
                  _const_spec(b_gate.shape), _const_spec(gq.shape), _const_spec(gk.shape),
                  _const_spec(ones_bd.shape)],
        out_specs=out_specs,
        out_shape=out_shape,
        scratch_shapes=[pltpu.VMEM((DIL_GROUP_W // LANES, tm, LANES), F32)],
        compiler_params=_params(("arbitrary",)),
        name="in_proj",
    )(x2, g1, w_in, b_gate, gq, gk, ones_bd)


def _sb_kernel(q_ref, k_ref, v_ref, mj_ref, o_ref, c_scr, acc_scr, *, tq):
    qi = pl.program_id(2)
    lane = lax.broadcasted_iota(jnp.int32, (tq, LANES), 1)
    row = lax.broadcasted_iota(jnp.int32, (tq, tq), 0)
    col = lax.broadcasted_iota(jnp.int32, (tq, tq), 1)
    causal = col < row
    q = q_ref[0]
    nh = tq // LANES

    def block(qm, kb, masked):
        start = pl.multiple_of(kb * tq, tq)
        kblk = k_ref[0, pl.ds(start, tq), :]
        vblk = v_ref[0, pl.ds(start, tq), :]
        z = lax.dot_general(qm, kblk, (((1,), (1,)), ((), ())), preferred_element_type=F32)
        sp = jnp.log(1.0 + jnp.exp(-jnp.abs(z)))
        ls = jnp.minimum(-z, 0.0) - sp
        lb = ls + z
        if masked:
            ls = jnp.where(causal, ls, 0.0)
        hi = ls.astype(BF16)
        lo = (ls - hi.astype(F32)).astype(BF16)
        c = c_scr[...]
        parts = [None] * nh
        for hf in reversed(range(nh)):
            sl = slice(hf * LANES, (hf + 1) * LANES)
            lhs = jnp.concatenate([hi[:, sl], lo[:, sl]], axis=1)
            r = jnp.dot(lhs, mj_ref[...], preferred_element_type=F32)
            a = jnp.exp(lb[:, sl] + c + r[:, :LANES])
            c = c + r[:, LANES:]
            if masked:
                a = jnp.where(causal[:, sl], a, 0.0)
            parts[hf] = a.astype(BF16)
        c_scr[...] = c
        a = jnp.concatenate(parts, axis=1)
        acc_scr[...] += jnp.dot(a, vblk, preferred_element_type=F32)

    outs = []
    for hh in range(2):
        qm = jnp.where((lane >= hh * HEAD_DIM) & (lane < (hh + 1) * HEAD_DIM), q, jnp.zeros_like(q))
        c_scr[...] = jnp.zeros_like(c_scr)
        acc_scr[...] = jnp.zeros_like(acc_scr)
        block(qm, qi, True)

        def body(it, carry, qm=qm):
            block(qm, qi - 1 - it, False)
            return carry

        lax.fori_loop(0, qi, body, 0)
        outs.append(acc_scr[...])
    o_ref[0] = jnp.where(lane < HEAD_DIM, outs[0], outs[1]).astype(BF16)


def _sb_attention(q, k, v, mj, *, batch, seq, tq):
    q3, k3, v3 = (t.reshape(batch, seq, SB_W) for t in (q, k, v))
    pairs = SB_W // LANES
    out = pl.pallas_call(
        functools.partial(_sb_kernel, tq=tq),
        grid=(batch, pairs, seq // tq),
        in_specs=[pl.BlockSpec((1, tq, LANES), lambda b, p, i: (b, i, p)),
                  pl.BlockSpec((1, seq, LANES), lambda b, p, i: (b, 0, p)),
                  pl.BlockSpec((1, seq, LANES), lambda b, p, i: (b, 0, p)),
                  _const_spec(mj.shape)],
        out_specs=pl.BlockSpec((1, tq, LANES), lambda b, p, i: (b, i, p)),
        out_shape=jax.ShapeDtypeStruct((batch, seq, SB_W), BF16),
        scratch_shapes=[pltpu.VMEM((tq, LANES), F32), pltpu.VMEM((tq, LANES), F32)],
        compiler_params=_params(("arbitrary", "arbitrary", "arbitrary")),
        name="sb_attn",
    )(q3, k3, v3, mj)
    return out.reshape(batch * seq, SB_W)


def _dil_kernel(*refs, seq, slopes):
    n = pl.program_id(1)
    ng = len(DIL_GROUPS)
    q_refs, kc_refs, kp_refs, vc_refs, vp_refs = (refs[i * ng:(i + 1) * ng] for i in range(5))
    o_refs, l_refs = refs[5 * ng:6 * ng], refs[6 * ng:7 * ng]
    tb = DIL_BLOCK
    lane = lax.broadcasted_iota(jnp.int32, (tb, LANES), 1)
    row = lax.broadcasted_iota(jnp.int32, (tb, tb), 0)
    col = lax.broadcasted_iota(jnp.int32, (tb, tb), 1)
    d_cur = row - col
    d_prev = d_cur + tb
    off_cur = d_cur.astype(F32)
    off_prev = d_prev.astype(F32)
    contract = (((1,), (1,)), ((), ()))
    for g, (window, dil) in enumerate(DIL_GROUPS):
        blocks_per_class = seq // dil // tb
        has_prev = (n % blocks_per_class) != 0
        valid_cur = d_cur >= 0
        valid_prev = (d_prev <= window // dil) & has_prev
        q, kc, kp, vc, vp = (r[0] for r in (q_refs[g], kc_refs[g], kp_refs[g], vc_refs[g], vp_refs[g]))
        for jp in range(DIL_GROUP_W // LANES):
            sl = slice(jp * LANES, (jp + 1) * LANES)
            o_h, l_h = [], []
            for hh in range(2):
                slope = float(slopes[g * DIL_HEADS_PER_GROUP + 2 * jp + hh]) * dil
                in_head = (lane >= hh * HEAD_DIM) & (lane < (hh + 1) * HEAD_DIM)
                qm = jnp.where(in_head, q[:, sl], jnp.zeros_like(q[:, sl]))
                zc = lax.dot_general(qm, kc[:, sl], contract, preferred_element_type=F32)
                zp = lax.dot_general(qm, kp[:, sl], contract, preferred_element_type=F32)
                zc = jnp.where(valid_cur, zc - slope * off_cur, NEG_BIG)
                zp = jnp.where(valid_prev, zp - slope * off_prev, NEG_BIG)
                m = jnp.maximum(jnp.max(zc, axis=1, keepdims=True), jnp.max(zp, axis=1, keepdims=True))
                pc = jnp.exp(zc - m)
                pp = jnp.exp(zp - m)
                den = jnp.sum(pc, axis=1, keepdims=True) + jnp.sum(pp, axis=1, keepdims=True)
                acc = (jnp.dot(pc.astype(BF16), vc[:, sl], preferred_element_type=F32)
                       + jnp.dot(pp.astype(BF16), vp[:, sl], preferred_element_type=F32))
                o_h.append(acc / den)
                l_h.append(jnp.broadcast_to(m + jnp.log(den), (tb, LANES)))
            first = lane < HEAD_DIM
            o_refs[g][0, :, sl] = jnp.where(first, o_h[0], o_h[1]).astype(BF16)
            l_refs[g][0, :, sl] = jnp.where(first, l_h[0], l_h[1])


def _dil_attention(qs, ks, vs, *, batch, seq):
    tb = DIL_BLOCK
    flat = lambda t: t.reshape(batch, seq, DIL_GROUP_W)
    qs, ks, vs = ([flat(t) for t in ts] for ts in (qs, ks, vs))
    cur = pl.BlockSpec((1, tb, DIL_GROUP_W), lambda b, n: (b, n, 0))
    prev = pl.BlockSpec((1, tb, DIL_GROUP_W), lambda b, n: (b, jnp.maximum(n - 1, 0), 0))
    ng = len(DIL_GROUPS)
    outs = pl.pallas_call(
        functools.partial(_dil_kernel, seq=seq, slopes=_slopes()),
        grid=(batch, seq // tb),
        in_specs=[cur] * ng + [cur] * ng + [prev] * ng + [cur] * ng + [prev] * ng,
        out_specs=[cur] * (2 * ng),
        out_shape=([jax.ShapeDtypeStruct((batch, seq, DIL_GROUP_W), BF16)] * ng
                   + [jax.ShapeDtypeStruct((batch, seq, DIL_GROUP_W), F32)] * ng),
        compiler_params=_params(("arbitrary", "arbitrary")),
        name="dil_attn",
    )(*qs, *ks, *ks, *vs, *vs)
    return outs[:ng], outs[ng:]


def _mix_kernel(x_ref, osb_ref, gate_ref, o0_ref, o1_ref, o2_ref, l0_ref, l1_ref, l2_ref,
                wsb_ref, wdl_ref, wout_ref, g2_ref, wq_ref, sk_ref,
                x1_ref, ht_ref, st_ref, o_scr, l_scr, *, tm):
    def natural(o_ref, l_ref, g):
        r = DIL_GROUPS[g][1]
        if r == 1:
            return o_ref[0].astype(F32), l_ref[0]
        nh = DIL_GROUP_W // LANES
        for c in range(r):
            for hf in range(nh):
                sl = slice(hf * LANES, (hf + 1) * LANES)
                o_scr[hf, pl.ds(c, tm // r, stride=r), :] = o_ref[0, c, :, sl].astype(F32)
                l_scr[hf, pl.ds(c, tm // r, stride=r), :] = l_ref[0, c, :, sl]
        return (jnp.concatenate([o_scr[hf] for hf in range(nh)], axis=1),
                jnp.concatenate([l_scr[hf] for hf in range(nh)], axis=1))

    o0, l0 = natural(o0_ref, l0_ref, 0)
    o1, l1 = natural(o1_ref, l1_ref, 1)
    o2, l2 = natural(o2_ref, l2_ref, 2)
    mx = jnp.maximum(jnp.maximum(l0, l1), l2)
    w0, w1, w2 = jnp.exp(l0 - mx), jnp.exp(l1 - mx), jnp.exp(l2 - mx)
    o_dl = (w0 * o0 + w1 * o1 + w2 * o2) / (w0 + w1 + w2)

    d = x_ref.shape[-1]
    y_sb = jnp.dot(osb_ref[...], wsb_ref[...], preferred_element_type=F32)
    y_dl = jnp.dot(o_dl.astype(BF16), wdl_ref[...], preferred_element_type=F32)
    mixed = gate_ref[:, :d].astype(F32) * y_sb + gate_ref[:, d:].astype(F32) * y_dl
    x1 = x_ref[...] + jnp.dot(mixed.astype(BF16), wout_ref[...], preferred_element_type=F32)
    x1_ref[...] = x1

    ms = jnp.mean(x1 * x1, axis=-1, keepdims=True)
    h = x1 * lax.rsqrt(ms + RMS_EPS) * g2_ref[...]
    ht_ref[...] = h.T.astype(BF16)
    qp = jnp.dot(h.astype(BF16), wq_ref[...], preferred_element_type=F32).astype(BF16)
    for hp in range(2 * PEER_HEADS):
        st_ref[hp] = lax.dot_general(sk_ref[hp], qp[:, hp * PEER_N_KEYS:(hp + 1) * PEER_N_KEYS],
                                     (((1,), (1,)), ((), ())), preferred_element_type=F32)


def _mix(x2, o_sb, gates, o_dl, l_dl, w_sb, w_dl, w_out, g2, w_q, sub_keys, *, batch, seq, tm):
    T, D = x2.shape
    tiles_per_seq = seq // tm
    row = lambda i: (i, 0)

    def group_spec(r):
        if r == 1:
            return pl.BlockSpec((1, tm, DIL_GROUP_W), lambda i: (i // tiles_per_seq, i % tiles_per_seq, 0))
        return pl.BlockSpec((1, r, tm // r, DIL_GROUP_W),
                            lambda i: (i // tiles_per_seq, 0, i % tiles_per_seq, 0))

    def group_view(t, r):
        return t if r == 1 else t.reshape(batch, r, seq // r, DIL_GROUP_W)

    rs = [dil for _, dil in DIL_GROUPS]
    o_dl = [group_view(t, r) for t, r in zip(o_dl, rs)]
    l_dl = [group_view(t, r) for t, r in zip(l_dl, rs)]
    g_specs = [group_spec(r) for r in rs]
    nhp = 2 * PEER_HEADS
    return pl.pallas_call(
        functools.partial(_mix_kernel, tm=tm),
        grid=(T // tm,),
        in_specs=[pl.BlockSpec((tm, D), row), pl.BlockSpec((tm, SB_W), row),
                  pl.BlockSpec((tm, gates.shape[-1]), row)] + g_specs + g_specs
                 + [_const_spec(w.shape) for w in (w_sb, w_dl, w_out, g2, w_q, sub_keys)],
        out_specs=[pl.BlockSpec((tm, D), row), pl.BlockSpec((D, tm), lambda i: (0, i)),
                   pl.BlockSpec((nhp, PEER_N_KEYS, tm), lambda i: (0, 0, i))],
        out_shape=[jax.ShapeDtypeStruct((T, D), F32), jax.ShapeDtypeStruct((D, T), BF16),
                   jax.ShapeDtypeStruct((nhp, PEER_N_KEYS, T), F32)],
        scratch_shapes=[pltpu.VMEM((DIL_GROUP_W // LANES, tm, LANES), F32)] * 2,
        compiler_params=_params(("arbitrary",)),
        name="mix",
    )(x2, o_sb, gates, *o_dl, *l_dl, w_sb, w_dl, w_out, g2, w_q, sub_keys)


def _staircase():
    K = PEER_TOPK
    rows = [(0, b) for b in range(K)]
    for a in range(1, 8):
        rows += [(a, b) for b in range(8)]
    rows += [(a, 0) for a in range(8, K)]
    a = np.array([r[0] for r in rows])
    b = np.array([r[1] for r in rows])
    return a, b, (a + 1) * (b + 1) <= K


def _topk_kernel(st_ref, flat_ref, rank2_ref, p2_ref, n1_ref, p1_ref, *, tl):
    K = PEER_TOPK
    nk = PEER_N_KEYS
    key_id = lax.broadcasted_iota(jnp.int32, (nk, tl), 0).astype(F32)
    cand_a, cand_b, cand_ok = _staircase()
    flat = flat_ref[...]
    ok = flat >= 0.0

    def top16(s):
        rank = jnp.full((nk, tl), float(K), F32)
        vals = []
        for k in range(K):
            m = jnp.max(s, axis=0, keepdims=True)
            idx = jnp.min(jnp.where(s == m, key_id, float(nk)), axis=0, keepdims=True)
            hit = key_id == idx
            rank = jnp.where(hit, float(k), rank)
            s = jnp.where(hit, -jnp.inf, s)
            vals.append(m)
        return vals, rank

    for h in range(PEER_HEADS):
        s1 = st_ref[2 * h]
        s2 = st_ref[2 * h + 1]
        v1, rank1 = top16(s1)
        v2, rank2 = top16(s2)
        v2lo = jnp.concatenate(v2[:8], axis=0)
        v2hi = jnp.concatenate(v2[8:], axis=0)
        groups = [v1[0] + v2lo, v1[0] + v2hi] + [v1[a] + v2lo for a in range(1, 8)]
        groups.append(jnp.concatenate(v1[8:], axis=0) + v2[0])
        cand = jnp.where(ok, jnp.concatenate(groups, axis=0), -jnp.inf)
        best_max = v1[0] + v2[0]
        zsum = jnp.zeros((1, tl), F32)
        n1 = jnp.zeros((nk, tl), F32)
        for k in range(K):
            m = jnp.max(cand, axis=0, keepdims=True)
            f = jnp.min(jnp.where(cand == m, flat, float(K * K)), axis=0, keepdims=True)
            cand = jnp.where(flat == f, -jnp.inf, cand)
            zsum = zsum + jnp.exp(m - best_max)
            a_sel = jnp.floor(f * (1.0 / K))
            n1 = n1 + jnp.where(rank1 == a_sel, 1.0, 0.0)
        rank2_ref[h] = rank2.astype(BF16)
        p2_ref[h] = jnp.exp(s2 - v2[0]).astype(BF16)
        n1_ref[h] = n1
        p1_ref[h] = jnp.exp(s1 - v1[0]) / zsum


def _peer_topk(s_t, *, tl):
    nhp, nk, T = s_t.shape
    a, b, okr = _staircase()
    flat = jnp.asarray(np.where(okr, a * PEER_TOPK + b, -1).astype(np.float32)[:, None])
    spec = pl.BlockSpec((PEER_HEADS, nk, tl), lambda i: (0, 0, i))
    return pl.pallas_call(
        functools.partial(_topk_kernel, tl=tl),
        grid=(T // tl,),
        in_specs=[pl.BlockSpec((nhp, nk, tl), lambda i: (0, 0, i)), _const_spec(flat.shape)],
        out_specs=[spec] * 4,
        out_shape=[jax.ShapeDtypeStruct((PEER_HEADS, nk, T), BF16)] * 2
                  + [jax.ShapeDtypeStruct((PEER_HEADS, nk, T), F32)] * 2,
        compiler_params=_params(("arbitrary",)),
        name="peer_topk",
    )(s_t, flat)


def _ffn_kernel(ht_ref, u_ref, vt_ref, rank2_ref, p2_ref, n1_ref, p1_ref, x1_ref,
                out_ref, a_scr, h_scr, acc_scr, *, et, tt, lt):
    e = pl.program_id(1)
    nk = PEER_N_KEYS

    @pl.when(e == 0)
    def _():
        acc_scr[...] = jnp.zeros_like(acc_scr)

    a_scr[...] = jnp.dot(u_ref[...], ht_ref[...], preferred_element_type=F32)
    i1_base = e * (et // nk)
    for j in range(et // nk):
        i1 = i1_base + j
        for t0 in range(0, tt, lt):
            ts = slice(t0, t0 + lt)
            gate = jnp.zeros((nk, lt), BF16)
            for h in range(PEER_HEADS):
                cnt = n1_ref[h, pl.ds(i1, 1), ts].astype(BF16)
                p1 = p1_ref[h, pl.ds(i1, 1), ts].astype(BF16)
                sel = jnp.where(rank2_ref[h, :, ts] < cnt, p2_ref[h, :, ts], jnp.zeros((nk, lt), BF16))
                gate = gate + sel * p1
            a = a_scr[j * nk:(j + 1) * nk, ts]
            act = 0.5 * a * (1.0 + lax.erf(a * math.sqrt(0.5)))
            h_scr[j * nk:(j + 1) * nk, ts] = act.astype(BF16) * gate
    acc_scr[...] += jnp.dot(vt_ref[...], h_scr[...], preferred_element_type=F32)

    @pl.when(e == pl.num_programs(1) - 1)
    def _():
        out_ref[...] = x1_ref[...] + acc_scr[...].T


def _peer_ffn(h_t, u, v_t, rank2, p2, n1, p1, x1, *, tt, et, lt):
    D, T = h_t.shape
    E = u.shape[0]
    nk = PEER_N_KEYS
    tab = pl.BlockSpec((PEER_HEADS, nk, tt), lambda t, e: (0, 0, t))
    return pl.pallas_call(
        functools.partial(_ffn_kernel, et=et, tt=tt, lt=lt),
        grid=(T // tt, E // et),
        in_specs=[pl.BlockSpec((D, tt), lambda t, e: (0, t)),
                  pl.BlockSpec((et, D), lambda t, e: (e, 0)),
                  pl.BlockSpec((D, et), lambda t, e: (0, e)),
                  tab, tab, tab, tab,
                  pl.BlockSpec((tt, D), lambda t, e: (t, 0))],
        out_specs=pl.BlockSpec((tt, D), lambda t, e: (t, 0)),
        out_shape=jax.ShapeDtypeStruct((T, D), F32),
        scratch_shapes=[pltpu.VMEM((et, tt), F32), pltpu.VMEM((et, tt), BF16), pltpu.VMEM((D, tt), F32)],
        compiler_params=_params(("arbitrary", "arbitrary")),
        name="peer_ffn",
    )(h_t, u, v_t, rank2, p2, n1, p1, x1)


def _tiles(batch, seq):
    coarsest = DIL_GROUPS[-1][1] * DIL_BLOCK
    assert seq % coarsest == 0, "sequence must hold whole blocks of the most dilated group"
    tokens = batch * seq
    tm = min(512, seq)
    return dict(tm_in=tm, tm_mix=min(256, seq), tq=256, tl=128,
                tt=min(512, tokens), et=512, lt=256)


def _suffix_ones_matrix():
    s = np.arange(LANES)
    tri = (s[:, None] > s[None, :]).astype(np.float32)
    half = np.concatenate([tri, np.ones((LANES, LANES), np.float32)], axis=1)
    return jnp.asarray(np.concatenate([half, half], axis=0), BF16)


def _block_diag_ones():
    h = np.arange(DIL_W) // HEAD_DIM
    return jnp.asarray((h[:, None] == h[None, :]).astype(np.float32), BF16)


def kernel(x, norm1_gain, w_in, b_gate, q_norm_gain, k_norm_gain, w_sb_out, w_dil_out, w_out,
           norm2_gain, w_peer_q, peer_sub_keys, peer_u, peer_v):
    B, S, D = x.shape
    depth = w_in.shape[0]
    t = _tiles(B, S)
    mj = _suffix_ones_matrix()
    ones_bd = _block_diag_ones()
    x2 = x.reshape(B * S, D)
    for l in range(depth):
        outs = _in_proj(
            x2, norm1_gain[l][None, :], w_in[l].astype(BF16), b_gate[l][None, :],
            q_norm_gain[l].reshape(1, DIL_W), k_norm_gain[l].reshape(1, DIL_W), ones_bd,
            batch=B, seq=S, tm=t["tm_in"])
        q_sb, k_sb, v_sb = outs[0:3]
        q_dl, k_dl, v_dl, gates = outs[3:6], outs[6:9], outs[9:12], outs[12]
        o_sb = _sb_attention(q_sb, k_sb, v_sb, mj, batch=B, seq=S, tq=t["tq"])
        o_dl, l_dl = _dil_attention(q_dl, k_dl, v_dl, batch=B, seq=S)
        nhp = 2 * PEER_HEADS
        x1, h_t, s_t = _mix(
            x2, o_sb, gates, o_dl, l_dl, w_sb_out[l].astype(BF16), w_dil_out[l].astype(BF16),
            w_out[l].astype(BF16), norm2_gain[l][None, :], w_peer_q[l].astype(BF16),
            peer_sub_keys[l].reshape(nhp, PEER_N_KEYS, -1).astype(BF16),
            batch=B, seq=S, tm=t["tm_mix"])
        rank2, p2, n1, p1 = _peer_topk(s_t, tl=t["tl"])
        x2 = _peer_ffn(h_t, peer_u[l].astype(BF16), peer_v[l].T.astype(BF16), rank2, p2, n1, p1, x1,
                       tt=t["tt"], et=t["et"], lt=t["lt"])
    return x2.reshape(B, S, D)
```

```python
import functools
import math

import numpy as np
import jax
import jax.numpy as jnp
from jax import lax
from jax.experimental import pallas as pl
from jax.experimental.pallas import tpu as pltpu

F32 = jnp.float32
BF16 = jnp.bfloat16

HEAD_DIM = 64
SB_HEADS = 8
DIL_GROUPS = ((128, 1), (512, 4), (2048, 16))
DIL_HEADS_PER_GROUP = 4
DIL_HEADS = DIL_HEADS_PER_GROUP * len(DIL_GROUPS)
PEER_HEADS = 8
PEER_N_KEYS = 128
PEER_TOPK = 16
RMS_EPS = 1e-6
ALIBI_MAX_EXP = 8.0
SB_W = SB_HEADS * HEAD_DIM
DIL_W = DIL_HEADS * HEAD_DIM
DIL_GROUP_W = DIL_HEADS_PER_GROUP * HEAD_DIM
QK_SCALE = HEAD_DIM ** -0.5
NEG_BIG = -1e30

LANES = 128
DIL_BLOCK = 128
VMEM_LIMIT = 56 * 1024 * 1024


def _slopes():
    h = np.arange(1, DIL_HEADS + 1, dtype=np.float32)
    return np.float32(2.0) ** (np.float32(-ALIBI_MAX_EXP) * h / np.float32(DIL_HEADS))


def _params(sem, flags=None):
    return pltpu.CompilerParams(dimension_semantics=sem, vmem_limit_bytes=VMEM_LIMIT, flags=flags)


def _const_spec(shape):
    return pl.BlockSpec(shape, lambda *_: (0,) * len(shape))


def _in_proj_kernel(x_ref, g1_ref, w_ref, bg_ref, gq_ref, gk_ref, ones_ref,
                    qsb_ref, ksb_ref, vsb_ref,
                    qd0_ref, qd1_ref, qd2_ref, kd0_ref, kd1_ref, kd2_ref,
                    vd0_ref, vd1_ref, vd2_ref, gate_ref, scr_ref, *, tm):
    x = x_ref[...]
    ms = jnp.mean(x * x, axis=-1, keepdims=True)
    xn = (x * lax.rsqrt(ms + RMS_EPS) * g1_ref[...]).astype(BF16)

    def proj(lo, width):
        return jnp.dot(xn, w_ref[:, lo:lo + width], preferred_element_type=F32)

    qsb_ref[...] = (proj(0, SB_W) * (QK_SCALE * math.log2(math.e))).astype(BF16)
    ksb_ref[...] = proj(SB_W, SB_W).astype(BF16)
    vsb_ref[...] = proj(2 * SB_W, SB_W).astype(BF16)

    def head_norm(t, gain):
        ssq = jnp.dot((t * t).astype(BF16), ones_ref[...], preferred_element_type=F32)
        return t * lax.rsqrt(ssq * (1.0 / HEAD_DIM) + RMS_EPS) * gain

    def split_groups(t, outs):
        outs[0][0] = t[:, :DIL_GROUP_W].astype(BF16)
        for g in (1, 2):
            r = DIL_GROUPS[g][1]
            for hf in range(DIL_GROUP_W // LANES):
                lo = g * DIL_GROUP_W + hf * LANES
                scr_ref[hf] = t[:, lo:lo + LANES]
            for c in range(r):
                for hf in range(DIL_GROUP_W // LANES):
                    rows = scr_ref[hf, pl.ds(c, tm // r, stride=r), :]
                    outs[g][0, c, :, hf * LANES:(hf + 1) * LANES] = rows.astype(BF16)

    base = 3 * SB_W
    q = head_norm(proj(base, DIL_W), gq_ref[...]) * QK_SCALE
    split_groups(q, (qd0_ref, qd1_ref, qd2_ref))
    k = head_norm(proj(base + DIL_W, DIL_W), gk_ref[...])
    split_groups(k, (kd0_ref, kd1_ref, kd2_ref))
    split_groups(proj(base + 2 * DIL_W, DIL_W), (vd0_ref, vd1_ref, vd2_ref))

    gw = gate_ref.shape[-1]
    gate_ref[...] = jax.nn.sigmoid(proj(base + 3 * DIL_W, gw) + bg_ref[...]).astype(BF16)


def _in_proj(x2, g1, w_in, b_gate, gq, gk, ones_bd, *, batch, seq, tm):
    T, D = x2.shape
    tiles_per_seq = seq // tm
    gw = b_gate.shape[-1]
    row = lambda i: (i, 0)
    nat = pl.BlockSpec((1, tm, DIL_GROUP_W), lambda i: (i // tiles_per_seq, i % tiles_per_seq, 0))

    def perm_spec(r):
        return pl.BlockSpec((1, r, tm // r, DIL_GROUP_W),
                            lambda i: (i // tiles_per_seq, 0, i % tiles_per_seq, 0))

    def perm_shape(r):
        return jax.ShapeDtypeStruct((batch, r, seq // r, DIL_GROUP_W), BF16)

    r1, r2 = DIL_GROUPS[1][1], DIL_GROUPS[2][1]
    dil_shapes = [jax.ShapeDtypeStruct((batch, seq, DIL_GROUP_W), BF16), perm_shape(r1), perm_shape(r2)]
    dil_specs = [nat, perm_spec(r1), perm_spec(r2)]
    out_shape = ([jax.ShapeDtypeStruct((T, SB_W), BF16)] * 3 + dil_shapes * 3
                 + [jax.ShapeDtypeStruct((T, gw), BF16)])
    out_specs = ([pl.BlockSpec((tm, SB_W), row)] * 3 + dil_specs * 3 + [pl.BlockSpec((tm, gw), row)])
    return pl.pallas_call(
        functools.partial(_in_proj_kernel, tm=tm),
        grid=(T // tm,),
        in_specs=[pl.BlockSpec((tm, D), row), _const_spec(g1.shape), _const_spec(w_in.shape),
                  _const_spec(b_gate.shape), _const_spec(gq.shape), _const_spec(gk.shape),
                  _const_spec(ones_bd.shape)],
        out_specs=out_specs,
        out_shape=out_shape,
        scratch_shapes=[pltpu.VMEM((DIL_GROUP_W // LANES, tm, LANES), F32)],
        compiler_params=_params(("arbitrary",)),
        name="in_proj",
    )(x2, g1, w_in, b_gate, gq, gk, ones_bd)


def _sb_kernel(q_ref, k_ref, v_ref, mj_ref, o_ref, q_scr, c_scr, acc_scr, *, tq, unroll):
    qi = pl.program_id(2)
    lane = lax.broadcasted_iota(jnp.int32, (tq, LANES), 1)
    q = q_ref[0]
    nh = tq // LANES
    sign = jnp.uint32(0x80000000)
    for hh in range(2):
        in_head = (lane >= hh * HEAD_DIM) & (lane < (hh + 1) * HEAD_DIM)
        q_scr[hh] = jnp.where(in_head, q, jnp.zeros_like(q))

    def chain(hh, kb, c, acc, diagonal):
        start = pl.multiple_of(kb * tq, tq)
        kblk = k_ref[0, pl.ds(start, tq), :]
        vblk = v_ref[0, pl.ds(start, tq), :]
        z = lax.dot_general(q_scr[hh], kblk, (((1,), (1,)), ((), ())), preferred_element_type=F32)
        neg_abs = lax.bitcast_convert_type(lax.bitcast_convert_type(z, jnp.uint32) | sign, F32)
        t = jnp.maximum(z, 0.0) + jnp.log(1.0 + jnp.exp2(neg_abs)) * math.log2(math.e)
        lb = z - t
        if diagonal:
            causal = (lax.broadcasted_iota(jnp.int32, (tq, tq), 1)
                      < lax.broadcasted_iota(jnp.int32, (tq, tq), 0))
            t = jnp.where(causal, t, 0.0)
            lb = jnp.where(causal, lb, NEG_BIG)
        hi = t.astype(BF16)
        lo = (t - hi.astype(F32)).astype(BF16)
        parts = [None] * nh
        for hf in reversed(range(nh)):
            sl = slice(hf * LANES, (hf + 1) * LANES)
            lhs = jnp.concatenate([hi[:, sl], lo[:, sl]], axis=1)
            r = jnp.dot(lhs, mj_ref[...], preferred_element_type=F32)
            parts[hf] = jnp.exp2(lb[:, sl] - c - r[:, :LANES]).astype(BF16)
            c = c + r[:, LANES:]
        a = jnp.concatenate(parts, axis=1)
        return c, acc + jnp.dot(a, vblk, preferred_element_type=F32)

    def trip(first_kb, count, diagonal):
        for hh in range(2):
            c, acc = c_scr[hh], acc_scr[hh]
            for u in range(count):
                c, acc = chain(hh, first_kb - u, c, acc, diagonal)
            c_scr[hh], acc_scr[hh] = c, acc

    c_scr[...] = jnp.zeros_like(c_scr)
    acc_scr[...] = jnp.zeros_like(acc_scr)
    trip(qi, 1, True)
    groups = qi // unroll

    def grouped(it, carry):
        trip(qi - 1 - it * unroll, unroll, False)
        return carry

    def single(it, carry):
        trip(qi - 1 - groups * unroll - it, 1, False)
        return carry

    lax.fori_loop(0, groups, grouped, 0)
    lax.fori_loop(0, qi - groups * unroll, single, 0)
    o_ref[0] = jnp.where(lane < HEAD_DIM, acc_scr[0], acc_scr[1]).astype(BF16)


def _sb_attention(q, k, v, mj, *, batch, seq, tq, unroll):
    q3, k3, v3 = (t.reshape(batch, seq, SB_W) for t in (q, k, v))
    pairs = SB_W // LANES
    out = pl.pallas_call(
        functools.partial(_sb_kernel, tq=tq, unroll=unroll),
        grid=(batch, pairs, seq // tq),
        in_specs=[pl.BlockSpec((1, tq, LANES), lambda b, p, i: (b, i, p)),
                  pl.BlockSpec((1, seq, LANES), lambda b, p, i: (b, 0, p)),
                  pl.BlockSpec((1, seq, LANES), lambda b, p, i: (b, 0, p)),
                  _const_spec(mj.shape)],
        out_specs=pl.BlockSpec((1, tq, LANES), lambda b, p, i: (b, i, p)),
        out_shape=jax.ShapeDtypeStruct((batch, seq, SB_W), BF16),
        scratch_shapes=[pltpu.VMEM((2, tq, LANES), BF16), pltpu.VMEM((2, tq, LANES), F32),
                        pltpu.VMEM((2, tq, LANES), F32)],
        compiler_params=_params(("arbitrary", "arbitrary", "arbitrary")),
        name="sb_attn",
    )(q3, k3, v3, mj)
    return out.reshape(batch * seq, SB_W)


def _dil_kernel(*refs, seq, slopes):
    n = pl.program_id(1)
    ng = len(DIL_GROUPS)
    q_refs, kc_refs, kp_refs, vc_refs, vp_refs = (refs[i * ng:(i + 1) * ng] for i in range(5))
    o_refs, l_refs = refs[5 * ng:6 * ng], refs[6 * ng:7 * ng]
    tb = DIL_BLOCK
    lane = lax.broadcasted_iota(jnp.int32, (tb, LANES), 1)
    row = lax.broadcasted_iota(jnp.int32, (tb, tb), 0)
    col = lax.broadcasted_iota(jnp.int32, (tb, tb), 1)
    d_cur = row - col
    d_prev = d_cur + tb
    off_cur = d_cur.astype(F32)
    off_prev = d_prev.astype(F32)
    contract = (((1,), (1,)), ((), ()))
    for g, (window, dil) in enumerate(DIL_GROUPS):
        blocks_per_class = seq // dil // tb
        has_prev = (n % blocks_per_class) != 0
        valid_cur = d_cur >= 0
        valid_prev = (d_prev <= window // dil) & has_prev
        q, kc, kp, vc, vp = (r[0] for r in (q_refs[g], kc_refs[g], kp_refs[g], vc_refs[g], vp_refs[g]))
        for jp in range(DIL_GROUP_W // LANES):
            sl = slice(jp * LANES, (jp + 1) * LANES)
            o_h, l_h = [], []
            for hh in range(2):
                slope = float(slopes[g * DIL_HEADS_PER_GROUP + 2 * jp + hh]) * dil
                in_head = (lane >= hh * HEAD_DIM) & (lane < (hh + 1) * HEAD_DIM)
                qm = jnp.where(in_head, q[:, sl], jnp.zeros_like(q[:, sl]))
                zc = lax.dot_general(qm, kc[:, sl], contract, preferred_element_type=F32)
                zp = lax.dot_general(qm, kp[:, sl], contract, preferred_element_type=F32)
                zc = jnp.where(valid_cur, zc - slope * off_cur, NEG_BIG)
                zp = jnp.where(valid_prev, zp - slope * off_prev, NEG_BIG)
                m = jnp.maximum(jnp.max(zc, axis=1, keepdims=True), jnp.max(zp, axis=1, keepdims=True))
                pc = jnp.exp(zc - m)
                pp = jnp.exp(zp - m)
                den = jnp.sum(pc, axis=1, keepdims=True) + jnp.sum(pp, axis=1, keepdims=True)
                acc = (jnp.dot(pc.astype(BF16), vc[:, sl], preferred_element_type=F32)
                       + jnp.dot(pp.astype(BF16), vp[:, sl], preferred_element_type=F32))
                o_h.append(acc / den)
                l_h.append(jnp.broadcast_to(m + jnp.log(den), (tb, LANES)))
            first = lane < HEAD_DIM
            o_refs[g][0, :, sl] = jnp.where(first, o_h[0], o_h[1]).astype(BF16)
            l_refs[g][0, :, sl] = jnp.where(first, l_h[0], l_h[1])


def _dil_attention(qs, ks, vs, *, batch, seq):
    tb = DIL_BLOCK
    flat = lambda t: t.reshape(batch, seq, DIL_GROUP_W)
    qs, ks, vs = ([flat(t) for t in ts] for ts in (qs, ks, vs))
    cur = pl.BlockSpec((1, tb, DIL_GROUP_W), lambda b, n: (b, n, 0))
    prev = pl.BlockSpec((1, tb, DIL_GROUP_W), lambda b, n: (b, jnp.maximum(n - 1, 0), 0))
    ng = len(DIL_GROUPS)
    outs = pl.pallas_call(
        functools.partial(_dil_kernel, seq=seq, slopes=_slopes()),
        grid=(batch, seq // tb),
        in_specs=[cur] * ng + [cur] * ng + [prev] * ng + [cur] * ng + [prev] * ng,
        out_specs=[cur] * (2 * ng),
        out_shape=([jax.ShapeDtypeStruct((batch, seq, DIL_GROUP_W), BF16)] * ng
                   + [jax.ShapeDtypeStruct((batch, seq, DIL_GROUP_W), F32)] * ng),
        compiler_params=_params(("arbitrary", "arbitrary")),
        name="dil_attn",
    )(*qs, *ks, *ks, *vs, *vs)
    return outs[:ng], outs[ng:]


def _mix_kernel(x_ref, osb_ref, gate_ref, o0_ref, o1_ref, o2_ref, l0_ref, l1_ref, l2_ref,
                wsb_ref, wdl_ref, wout_ref, g2_ref, wq_ref, sk_ref,
                x1_ref, ht_ref, st_ref, o_scr, l_scr, *, tm):
    def natural(o_ref, l_ref, g):
        r = DIL_GROUPS[g][1]
        if r == 1:
            return o_ref[0].astype(F32), l_ref[0]
        nh = DIL_GROUP_W // LANES
        for c in range(r):
            for hf in range(nh):
                sl = slice(hf * LANES, (hf + 1) * LANES)
                o_scr[hf, pl.ds(c, tm // r, stride=r), :] = o_ref[0, c, :, sl].astype(F32)
                l_scr[hf, pl.ds(c, tm // r, stride=r), :] = l_ref[0, c, :, sl]
        return (jnp.concatenate([o_scr[hf] for hf in range(nh)], axis=1),
                jnp.concatenate([l_scr[hf] for hf in range(nh)], axis=1))

    o0, l0 = natural(o0_ref, l0_ref, 0)
    o1, l1 = natural(o1_ref, l1_ref, 1)
    o2, l2 = natural(o2_ref, l2_ref, 2)
    mx = jnp.maximum(jnp.maximum(l0, l1), l2)
    w0, w1, w2 = jnp.exp(l0 - mx), jnp.exp(l1 - mx), jnp.exp(l2 - mx)
    o_dl = (w0 * o0 + w1 * o1 + w2 * o2) / (w0 + w1 + w2)

    d = x_ref.shape[-1]
    y_sb = jnp.dot(osb_ref[...], wsb_ref[...], preferred_element_type=F32)
    y_dl = jnp.dot(o_dl.astype(BF16), wdl_ref[...], preferred_element_type=F32)
    mixed = gate_ref[:, :d].astype(F32) * y_sb + gate_ref[:, d:].astype(F32) * y_dl
    x1 = x_ref[...] + jnp.dot(mixed.astype(BF16), wout_ref[...], preferred_element_type=F32)
    x1_ref[...] = x1

    ms = jnp.mean(x1 * x1, axis=-1, keepdims=True)
    h = x1 * lax.rsqrt(ms + RMS_EPS) * g2_ref[...]
    ht_ref[...] = h.T.astype(BF16)
    qp = jnp.dot(h.astype(BF16), wq_ref[...], preferred_element_type=F32).astype(BF16)
    for hp in range(2 * PEER_HEADS):
        st_ref[hp] = lax.dot_general(sk_ref[hp], qp[:, hp * PEER_N_KEYS:(hp + 1) * PEER_N_KEYS],
                                     (((1,), (1,)), ((), ())), preferred_element_type=F32)


def _mix(x2, o_sb, gates, o_dl, l_dl, w_sb, w_dl, w_out, g2, w_q, sub_keys, *, batch, seq, tm):
    T, D = x2.shape
    tiles_per_seq = seq // tm
    row = lambda i: (i, 0)

    def group_spec(r):
        if r == 1:
            return pl.BlockSpec((1, tm, DIL_GROUP_W), lambda i: (i // tiles_per_seq, i % tiles_per_seq, 0))
        return pl.BlockSpec((1, r, tm // r, DIL_GROUP_W),
                            lambda i: (i // tiles_per_seq, 0, i % tiles_per_seq, 0))

    def group_view(t, r):
        return t if r == 1 else t.reshape(batch, r, seq // r, DIL_GROUP_W)

    rs = [dil for _, dil in DIL_GROUPS]
    o_dl = [group_view(t, r) for t, r in zip(o_dl, rs)]
    l_dl = [group_view(t, r) for t, r in zip(l_dl, rs)]
    g_specs = [group_spec(r) for r in rs]
    nhp = 2 * PEER_HEADS
    return pl.pallas_call(
        functools.partial(_mix_kernel, tm=tm),
        grid=(T // tm,),
        in_specs=[pl.BlockSpec((tm, D), row), pl.BlockSpec((tm, SB_W), row),
                  pl.BlockSpec((tm, gates.shape[-1]), row)] + g_specs + g_specs
                 + [_const_spec(w.shape) for w in (w_sb, w_dl, w_out, g2, w_q, sub_keys)],
        out_specs=[pl.BlockSpec((tm, D), row), pl.BlockSpec((D, tm), lambda i: (0, i)),
                   pl.BlockSpec((nhp, PEER_N_KEYS, tm), lambda i: (0, 0, i))],
        out_shape=[jax.ShapeDtypeStruct((T, D), F32), jax.ShapeDtypeStruct((D, T), BF16),
                   jax.ShapeDtypeStruct((nhp, PEER_N_KEYS, T), F32)],
        scratch_shapes=[pltpu.VMEM((DIL_GROUP_W // LANES, tm, LANES), F32)] * 2,
        compiler_params=_params(("arbitrary",)),
        name="mix",
    )(x2, o_sb, gates, *o_dl, *l_dl, w_sb, w_dl, w_out, g2, w_q, sub_keys)


def _staircase():
    K = PEER_TOPK
    rows = [(0, b) for b in range(K)]
    for a in range(1, 8):
        rows += [(a, b) for b in range(8)]
    rows += [(a, 0) for a in range(8, K)]
    a = np.array([r[0] for r in rows])
    b = np.array([r[1] for r in rows])
    return a, b, (a + 1) * (b + 1) <= K


def _twice_bf16(x):
    bits = lax.bitcast_convert_type(x.astype(BF16).astype(F32), jnp.uint32)
    return bits | (bits >> 16)


def _topk_kernel(st_ref, flat_ref, rank2_ref, p2_ref, n1_ref, p1_ref, *, tl):
    K = PEER_TOPK
    nk = PEER_N_KEYS
    key_id = lax.broadcasted_iota(jnp.int32, (nk, tl), 0).astype(F32)
    cand_a, cand_b, cand_ok = _staircase()
    flat = flat_ref[...]
    ok = flat >= 0.0

    def top16(s):
        rank = jnp.full((nk, tl), float(K), F32)
        vals = []
        for k in range(K):
            m = jnp.max(s, axis=0, keepdims=True)
            idx = jnp.min(jnp.where(s == m, key_id, float(nk)), axis=0, keepdims=True)
            hit = key_id == idx
            rank = jnp.where(hit, float(k), rank)
            s = jnp.where(hit, -jnp.inf, s)
            vals.append(m)
        return vals, rank

    for h in range(PEER_HEADS):
        s1 = st_ref[2 * h]
        s2 = st_ref[2 * h + 1]
        v1, rank1 = top16(s1)
        v2, rank2 = top16(s2)
        v2lo = jnp.concatenate(v2[:8], axis=0)
        v2hi = jnp.concatenate(v2[8:], axis=0)
        groups = [v1[0] + v2lo, v1[0] + v2hi] + [v1[a] + v2lo for a in range(1, 8)]
        groups.append(jnp.concatenate(v1[8:], axis=0) + v2[0])
        cand = jnp.where(ok, jnp.concatenate(groups, axis=0), -jnp.inf)
        best_max = v1[0] + v2[0]
        zsum = jnp.zeros((1, tl), F32)
        n1 = jnp.zeros((nk, tl), F32)
        for k in range(K):
            m = jnp.max(cand, axis=0, keepdims=True)
            f = jnp.min(jnp.where(cand == m, flat, float(K * K)), axis=0, keepdims=True)
            cand = jnp.where(flat == f, -jnp.inf, cand)
            zsum = zsum + jnp.exp(m - best_max)
            a_sel = jnp.floor(f * (1.0 / K))
            n1 = n1 + jnp.where(rank1 == a_sel, 1.0, 0.0)
        rank2_ref[h] = rank2.astype(BF16)
        p2_ref[h] = jnp.exp(s2 - v2[0]).astype(BF16)
        n1_ref[h] = _twice_bf16(n1)
        p1_ref[h] = _twice_bf16(0.5 * jnp.exp(s1 - v1[0]) / zsum)


def _peer_topk(s_t, *, tl):
    nhp, nk, T = s_t.shape
    a, b, okr = _staircase()
    flat = jnp.asarray(np.where(okr, a * PEER_TOPK + b, -1).astype(np.float32)[:, None])
    spec = pl.BlockSpec((PEER_HEADS, nk, tl), lambda i: (0, 0, i))
    return pl.pallas_call(
        functools.partial(_topk_kernel, tl=tl),
        grid=(T // tl,),
        in_specs=[pl.BlockSpec((nhp, nk, tl), lambda i: (0, 0, i)), _const_spec(flat.shape)],
        out_specs=[spec] * 4,
        out_shape=[jax.ShapeDtypeStruct((PEER_HEADS, nk, T), BF16)] * 2
                  + [jax.ShapeDtypeStruct((PEER_HEADS, nk, T), jnp.uint32)] * 2,
        compiler_params=_params(("arbitrary",)),
        name="peer_topk",
    )(s_t, flat)


def _ffn_kernel(ht_ref, u_ref, vt_ref, rank2_ref, p2_ref, n1_ref, p1_ref, x1_ref,
                out_ref, a_scr, h_scr, acc_scr, *, et, tt, lt, ec):
    e = pl.program_id(1)
    nk = PEER_N_KEYS
    rows = 16

    @pl.when(e == 0)
    def _():
        acc_scr[...] = jnp.zeros_like(acc_scr)

    def packed_row(ref, h, i1, ts):
        word = jnp.broadcast_to(ref[h, pl.ds(i1, 1), ts], (rows // 2, lt))
        return jnp.concatenate([pltpu.bitcast(word, BF16)] * (nk // rows), axis=0)

    ahead = a_scr.shape[0] - 1

    def expert_scores(c):
        if c < et // ec:
            a_scr[c % (ahead + 1)] = jnp.dot(u_ref[c * ec:(c + 1) * ec, :], ht_ref[...],
                                             preferred_element_type=F32)

    for c in range(ahead):
        expert_scores(c)
    group = h_scr.shape[1] // ec
    for c in range(et // ec):
        grp = c // group
        expert_scores(c + ahead)
        for j in range(ec // nk):
            i1 = e * (et // nk) + c * (ec // nk) + j
            rs = slice(j * nk, (j + 1) * nk)
            for t0 in range(0, tt, lt):
                ts = slice(t0, t0 + lt)
                half_gate = jnp.zeros((nk, lt), BF16)
                for h in range(PEER_HEADS):
                    cnt = packed_row(n1_ref, h, i1, ts)
                    p1 = packed_row(p1_ref, h, i1, ts)
                    sel = jnp.where(rank2_ref[h, :, ts] < cnt, p2_ref[h, :, ts], jnp.zeros((nk, lt), BF16))
                    half_gate = half_gate + sel * p1
                a = a_scr[c % (ahead + 1), rs, ts]
                twice_gelu = a + a * lax.erf(a * math.sqrt(0.5))
                h_scr[grp % 2, (c % group) * ec + j * nk:(c % group) * ec + (j + 1) * nk, ts] = (
                    twice_gelu.astype(BF16) * half_gate)
        if (c + 1) % group == 0:
            acc_scr[...] += jnp.dot(vt_ref[:, grp * group * ec:(grp + 1) * group * ec], h_scr[grp % 2],
                                    preferred_element_type=F32)

    @pl.when(e == pl.num_programs(1) - 1)
    def _():
        out_ref[...] = x1_ref[...] + acc_scr[...].T


def _peer_ffn(h_t, u, v_t, rank2, p2, n1, p1, x1, *, tt, et, lt, ec):
    D, T = h_t.shape
    E = u.shape[0]
    nk = PEER_N_KEYS
    tab = pl.BlockSpec((PEER_HEADS, nk, tt), lambda t, e: (0, 0, t))
    return pl.pallas_call(
        functools.partial(_ffn_kernel, et=et, tt=tt, lt=lt, ec=ec),
        grid=(T // tt, E // et),
        in_specs=[pl.BlockSpec((D, tt), lambda t, e: (0, t)),
                  pl.BlockSpec((et, D), lambda t, e: (e, 0)),
                  pl.BlockSpec((D, et), lambda t, e: (0, e)),
                  tab, tab, tab, tab,
                  pl.BlockSpec((tt, D), lambda t, e: (t, 0))],
        out_specs=pl.BlockSpec((tt, D), lambda t, e: (t, 0)),
        out_shape=jax.ShapeDtypeStruct((T, D), F32),
        scratch_shapes=[pltpu.VMEM((3, ec, tt), F32), pltpu.VMEM((2, 2 * ec, tt), BF16),
                        pltpu.VMEM((D, tt), F32)],
        compiler_params=_params(("arbitrary", "arbitrary")),
        name="peer_ffn",
    )(h_t, u, v_t, rank2, p2, n1, p1, x1)


def _tiles(batch, seq):
    coarsest = DIL_GROUPS[-1][1] * DIL_BLOCK
    assert seq % coarsest == 0, "sequence must hold whole blocks of the most dilated group"
    tokens = batch * seq
    tm = min(512, seq)
    return dict(tm_in=tm, tm_mix=min(256, seq), tq=256, sb_unroll=4, tl=128,
                tt=min(512, tokens), et=2048, ec=256, lt=256)


def _suffix_ones_matrix():
    s = np.arange(LANES)
    tri = (s[:, None] > s[None, :]).astype(np.float32)
    half = np.concatenate([tri, np.ones((LANES, LANES), np.float32)], axis=1)
    return jnp.asarray(np.concatenate([half, half], axis=0), BF16)


def _block_diag_ones():
    h = np.arange(DIL_W) // HEAD_DIM
    return jnp.asarray((h[:, None] == h[None, :]).astype(np.float32), BF16)


def kernel(x, norm1_gain, w_in, b_gate, q_norm_gain, k_norm_gain, w_sb_out, w_dil_out, w_out,
           norm2_gain, w_peer_q, peer_sub_keys, peer_u, peer_v):
    B, S, D = x.shape
    depth = w_in.shape[0]
    t = _tiles(B, S)
    mj = _suffix_ones_matrix()
    ones_bd = _block_diag_ones()
    x2 = x.reshape(B * S, D)
    for l in range(depth):
        outs = _in_proj(
            x2, norm1_gain[l][None, :], w_in[l].astype(BF16), b_gate[l][None, :],
            q_norm_gain[l].reshape(1, DIL_W), k_norm_gain[l].reshape(1, DIL_W), ones_bd,
            batch=B, seq=S, tm=t["tm_in"])
        q_sb, k_sb, v_sb = outs[0:3]
        q_dl, k_dl, v_dl, gates = outs[3:6], outs[6:9], outs[9:12], outs[12]
        o_sb = _sb_attention(q_sb, k_sb, v_sb, mj, batch=B, seq=S, tq=t["tq"], unroll=t["sb_unroll"])
        o_dl, l_dl = _dil_attention(q_dl, k_dl, v_dl, batch=B, seq=S)
        nhp = 2 * PEER_HEADS
        x1, h_t, s_t = _mix(
            x2, o_sb, gates, o_dl, l_dl, w_sb_out[l].astype(BF16), w_dil_out[l].astype(BF16),
            w_out[l].astype(BF16), norm2_gain[l][None, :], w_peer_q[l].astype(BF16),
            peer_sub_keys[l].reshape(nhp, PEER_N_KEYS, -1).astype(BF16),
            batch=B, seq=S, tm=t["tm_mix"])
        rank2, p2, n1, p1 = _peer_topk(s_t, tl=t["tl"])
        x2 = _peer_ffn(h_t, peer_u[l].astype(BF16), peer_v[l].T.astype(BF16), rank2, p2, n1, p1, x1,
                       tt=t["tt"], et=t["et"], lt=t["lt"], ec=t["ec"])
    return x2.reshape(B, S, D)
```

```python
import functools
import math

import numpy as np
import jax
import jax.numpy as jnp
from jax import lax
from jax.experimental import pallas as pl
from jax.experimental.pallas import tpu as pltpu

F32 = jnp.float32
BF16 = jnp.bfloat16

HEAD_DIM = 64
SB_HEADS = 8
DIL_GROUPS = ((128, 1), (512, 4), (2048, 16))
DIL_HEADS_PER_GROUP = 4
DIL_HEADS = DIL_HEADS_PER_GROUP * len(DIL_GROUPS)
PEER_HEADS = 8
PEER_N_KEYS = 128
PEER_TOPK = 16
RMS_EPS = 1e-6
ALIBI_MAX_EXP = 8.0
SB_W = SB_HEADS * HEAD_DIM
DIL_W = DIL_HEADS * HEAD_DIM
DIL_GROUP_W = DIL_HEADS_PER_GROUP * HEAD_DIM
QK_SCALE = HEAD_DIM ** -0.5
NEG_BIG = -1e30

LANES = 128
DIL_BLOCK = 128
VMEM_LIMIT = 56 * 1024 * 1024


def _slopes():
    h = np.arange(1, DIL_HEADS + 1, dtype=np.float32)
    return np.float32(2.0) ** (np.float32(-ALIBI_MAX_EXP) * h / np.float32(DIL_HEADS))


def _params(sem, flags=None):
    return pltpu.CompilerParams(dimension_semantics=sem, vmem_limit_bytes=VMEM_LIMIT, flags=flags)


def _const_spec(shape):
    return pl.BlockSpec(shape, lambda *_: (0,) * len(shape))


def _in_proj_kernel(x_ref, g1_ref, w_ref, bg_ref, gq_ref, gk_ref, ones_ref,
                    qsb_ref, ksb_ref, vsb_ref,
                    qd0_ref, qd1_ref, qd2_ref, kd0_ref, kd1_ref, kd2_ref,
                    vd0_ref, vd1_ref, vd2_ref, gate_ref, scr_ref, *, tm):
    x = x_ref[...]
    ms = jnp.mean(x * x, axis=-1, keepdims=True)
    xn = (x * lax.rsqrt(ms + RMS_EPS) * g1_ref[...]).astype(BF16)

    def proj(lo, width):
        return jnp.dot(xn, w_ref[:, lo:lo + width], preferred_element_type=F32)

    qsb_ref[...] = (proj(0, SB_W) * (QK_SCALE * math.log2(math.e))).astype(BF16)
    ksb_ref[...] = proj(SB_W, SB_W).astype(BF16)
    vsb_ref[...] = proj(2 * SB_W, SB_W).astype(BF16)

    def head_norm(t, gain):
        ssq = jnp.dot((t * t).astype(BF16), ones_ref[...], preferred_element_type=F32)
        return t * lax.rsqrt(ssq * (1.0 / HEAD_DIM) + RMS_EPS) * gain

    def split_groups(t, outs):
        outs[0][0] = t[:, :DIL_GROUP_W].astype(BF16)
        for g in (1, 2):
            r = DIL_GROUPS[g][1]
            for hf in range(DIL_GROUP_W // LANES):
                lo = g * DIL_GROUP_W + hf * LANES
                scr_ref[hf] = t[:, lo:lo + LANES]
            for c in range(r):
                for hf in range(DIL_GROUP_W // LANES):
                    rows = scr_ref[hf, pl.ds(c, tm // r, stride=r), :]
                    outs[g][0, c, :, hf * LANES:(hf + 1) * LANES] = rows.astype(BF16)

    base = 3 * SB_W
    q = head_norm(proj(base, DIL_W), gq_ref[...]) * QK_SCALE
    split_groups(q, (qd0_ref, qd1_ref, qd2_ref))
    k = head_norm(proj(base + DIL_W, DIL_W), gk_ref[...])
    split_groups(k, (kd0_ref, kd1_ref, kd2_ref))
    split_groups(proj(base + 2 * DIL_W, DIL_W), (vd0_ref, vd1_ref, vd2_ref))

    gw = gate_ref.shape[-1]
    gate_ref[...] = jax.nn.sigmoid(proj(base + 3 * DIL_W, gw) + bg_ref[...]).astype(BF16)


def _in_proj(x2, g1, w_in, b_gate, gq, gk, ones_bd, *, batch, seq, tm):
    T, D = x2.shape
    tiles_per_seq = seq // tm
    gw = b_gate.shape[-1]
    row = lambda i: (i, 0)
    nat = pl.BlockSpec((1, tm, DIL_GROUP_W), lambda i: (i // tiles_per_seq, i % tiles_per_seq, 0))

    def perm_spec(r):
        return pl.BlockSpec((1, r, tm // r, DIL_GROUP_W),
                            lambda i: (i // tiles_per_seq, 0, i % tiles_per_seq, 0))

    def perm_shape(r):
        return jax.ShapeDtypeStruct((batch, r, seq // r, DIL_GROUP_W), BF16)

    r1, r2 = DIL_GROUPS[1][1], DIL_GROUPS[2][1]
    dil_shapes = [jax.ShapeDtypeStruct((batch, seq, DIL_GROUP_W), BF16), perm_shape(r1), perm_shape(r2)]
    dil_specs = [nat, perm_spec(r1), perm_spec(r2)]
    out_shape = ([jax.ShapeDtypeStruct((T, SB_W), BF16)] * 3 + dil_shapes * 3
                 + [jax.ShapeDtypeStruct((T, gw), BF16)])
    out_specs = ([pl.BlockSpec((tm, SB_W), row)] * 3 + dil_specs * 3 + [pl.BlockSpec((tm, gw), row)])
    return pl.pallas_call(
        functools.partial(_in_proj_kernel, tm=tm),
        grid=(T // tm,),
        in_specs=[pl.BlockSpec((tm, D), row), _const_spec(g1.shape), _const_spec(w_in.shape),
                  _const_spec(b_gate.shape), _const_spec(gq.shape), _const_spec(gk.shape),
                  _const_spec(ones_bd.shape)],
        out_specs=out_specs,
        out_shape=out_shape,
        scratch_shapes=[pltpu.VMEM((DIL_GROUP_W // LANES, tm, LANES), F32)],
        compiler_params=_params(("arbitrary",)),
        name="in_proj",
    )(x2, g1, w_in, b_gate, gq, gk, ones_bd)


def _sb_kernel(q_ref, k_ref, v_ref, mj_ref, o_ref, q_scr, c_scr, acc_scr, *, tq, unroll):
    qi = pl.program_id(2)
    lane = lax.broadcasted_iota(jnp.int32, (tq, LANES), 1)
    q = q_ref[0]
    nh = tq // LANES
    sign = jnp.uint32(0x80000000)
    for hh in range(2):
        in_head = (lane >= hh * HEAD_DIM) & (lane < (hh + 1) * HEAD_DIM)
        q_scr[hh] = jnp.where(in_head, q, jnp.zeros_like(q))

    def trip(first_kb, count, diagonal):
        chains = [(hh, u) for u in range(count) for hh in range(2)]
        starts = [pl.multiple_of((first_kb - u) * tq, tq) for u in range(count)]
        zs = [lax.dot_general(q_scr[hh], k_ref[0, pl.ds(starts[u], tq), :], (((1,), (1,)), ((), ())),
                              preferred_element_type=F32) for hh, u in chains]
        lbs, lhss = [], []
        for z in zs:
            neg_abs = lax.bitcast_convert_type(lax.bitcast_convert_type(z, jnp.uint32) | sign, F32)
            t = jnp.maximum(z, 0.0) + jnp.log(1.0 + jnp.exp2(neg_abs)) * math.log2(math.e)
            lb = z - t
            if diagonal:
                causal = (lax.broadcasted_iota(jnp.int32, (tq, tq), 1)
                          < lax.broadcasted_iota(jnp.int32, (tq, tq), 0))
                t = jnp.where(causal, t, 0.0)
                lb = jnp.where(causal, lb, NEG_BIG)
            hi = t.astype(BF16)
            lo = (t - hi.astype(F32)).astype(BF16)
            lbs.append(lb)
            lhss.append([jnp.concatenate([hi[:, hf * LANES:(hf + 1) * LANES],
                                          lo[:, hf * LANES:(hf + 1) * LANES]], axis=1) for hf in range(nh)])
        sums = [[jnp.dot(lhs[hf], mj_ref[...], preferred_element_type=F32) for hf in range(nh)]
                for lhs in lhss]
        carried = [c_scr[0], c_scr[1]]
        weights = []
        for (hh, u), lb, r in zip(chains, lbs, sums):
            parts = [None] * nh
            for hf in reversed(range(nh)):
                sl = slice(hf * LANES, (hf + 1) * LANES)
                parts[hf] = jnp.exp2(lb[:, sl] - carried[hh] - r[hf][:, :LANES]).astype(BF16)
                carried[hh] = carried[hh] + r[hf][:, LANES:]
            weights.append(jnp.concatenate(parts, axis=1))
        c_scr[0], c_scr[1] = carried
        outs = [jnp.dot(a, v_ref[0, pl.ds(starts[u], tq), :], preferred_element_type=F32)
                for (hh, u), a in zip(chains, weights)]
        for hh in range(2):
            acc = acc_scr[hh]
            for (h2, u), o in zip(chains, outs):
                if h2 == hh:
                    acc = acc + o
            acc_scr[hh] = acc

    c_scr[...] = jnp.zeros_like(c_scr)
    acc_scr[...] = jnp.zeros_like(acc_scr)
    trip(qi, 1, True)
    groups = qi // unroll

    def grouped(it, carry):
        trip(qi - 1 - it * unroll, unroll, False)
        return carry

    def single(it, carry):
        trip(qi - 1 - groups * unroll - it, 1, False)
        return carry

    lax.fori_loop(0, groups, grouped, 0)
    lax.fori_loop(0, qi - groups * unroll, single, 0)
    o_ref[0] = jnp.where(lane < HEAD_DIM, acc_scr[0], acc_scr[1]).astype(BF16)


def _sb_attention(q, k, v, mj, *, batch, seq, tq, unroll):
    q3, k3, v3 = (t.reshape(batch, seq, SB_W) for t in (q, k, v))
    pairs = SB_W // LANES
    out = pl.pallas_call(
        functools.partial(_sb_kernel, tq=tq, unroll=unroll),
        grid=(batch, pairs, seq // tq),
        in_specs=[pl.BlockSpec((1, tq, LANES), lambda b, p, i: (b, i, p)),
                  pl.BlockSpec((1, seq, LANES), lambda b, p, i: (b, 0, p)),
                  pl.BlockSpec((1, seq, LANES), lambda b, p, i: (b, 0, p)),
                  _const_spec(mj.shape)],
        out_specs=pl.BlockSpec((1, tq, LANES), lambda b, p, i: (b, i, p)),
        out_shape=jax.ShapeDtypeStruct((batch, seq, SB_W), BF16),
        scratch_shapes=[pltpu.VMEM((2, tq, LANES), BF16), pltpu.VMEM((2, tq, LANES), F32),
                        pltpu.VMEM((2, tq, LANES), F32)],
        compiler_params=_params(("arbitrary", "arbitrary", "arbitrary")),
        name="sb_attn",
    )(q3, k3, v3, mj)
    return out.reshape(batch * seq, SB_W)


def _dil_kernel(*refs, seq, slopes):
    n = pl.program_id(1)
    ng = len(DIL_GROUPS)
    q_refs, kc_refs, kp_refs, vc_refs, vp_refs = (refs[i * ng:(i + 1) * ng] for i in range(5))
    o_refs, l_refs = refs[5 * ng:6 * ng], refs[6 * ng:7 * ng]
    tb = DIL_BLOCK
    lane = lax.broadcasted_iota(jnp.int32, (tb, LANES), 1)
    row = lax.broadcasted_iota(jnp.int32, (tb, tb), 0)
    col = lax.broadcasted_iota(jnp.int32, (tb, tb), 1)
    d_cur = row - col
    d_prev = d_cur + tb
    off_cur = d_cur.astype(F32)
    off_prev = d_prev.astype(F32)
    contract = (((1,), (1,)), ((), ()))
    valid_cur = d_cur >= 0
    heads = [(g, jp, hh) for g in range(ng) for jp in range(DIL_GROUP_W // LANES) for hh in range(2)]
    lanes_of = lambda jp: slice(jp * LANES, (jp + 1) * LANES)

    scores = []
    for g, jp, hh in heads:
        sl = lanes_of(jp)
        in_head = (lane >= hh * HEAD_DIM) & (lane < (hh + 1) * HEAD_DIM)
        q = q_refs[g][0, :, sl]
        qm = jnp.where(in_head, q, jnp.zeros_like(q))
        scores.append((lax.dot_general(qm, kc_refs[g][0, :, sl], contract, preferred_element_type=F32),
                       lax.dot_general(qm, kp_refs[g][0, :, sl], contract, preferred_element_type=F32)))

    masked = []
    for (g, jp, hh), (zc, zp) in zip(heads, scores):
        window, dil = DIL_GROUPS[g]
        has_prev = (n % (seq // dil // tb)) != 0
        valid_prev = (d_prev <= window // dil) & has_prev
        slope = float(slopes[g * DIL_HEADS_PER_GROUP + 2 * jp + hh]) * dil
        zc = jnp.where(valid_cur, zc - slope * off_cur, NEG_BIG)
        zp = jnp.where(valid_prev, zp - slope * off_prev, NEG_BIG)
        m = jnp.max(jnp.maximum(zc, zp), axis=1, keepdims=True)
        masked.append((zc, zp, m))

    weighted = []
    for (g, jp, hh), (zc, zp, m) in zip(heads, masked):
        sl = lanes_of(jp)
        pc = jnp.exp(zc - m)
        pp = jnp.exp(zp - m)
        den = jnp.sum(pc + pp, axis=1, keepdims=True)
        acc = (jnp.dot(pc.astype(BF16), vc_refs[g][0, :, sl], preferred_element_type=F32)
               + jnp.dot(pp.astype(BF16), vp_refs[g][0, :, sl], preferred_element_type=F32))
        weighted.append((acc, den, m))

    first = lane < HEAD_DIM
    for i in range(0, len(heads), 2):
        g, jp, _ = heads[i]
        sl = lanes_of(jp)
        (a0, d0, m0), (a1, d1, m1) = weighted[i], weighted[i + 1]
        o_refs[g][0, :, sl] = jnp.where(first, a0 / d0, a1 / d1).astype(BF16)
        l_refs[g][0, :, sl] = jnp.where(first, jnp.broadcast_to(m0 + jnp.log(d0), (tb, LANES)),
                                        jnp.broadcast_to(m1 + jnp.log(d1), (tb, LANES)))


def _dil_attention(qs, ks, vs, *, batch, seq):
    tb = DIL_BLOCK
    flat = lambda t: t.reshape(batch, seq, DIL_GROUP_W)
    qs, ks, vs = ([flat(t) for t in ts] for ts in (qs, ks, vs))
    cur = pl.BlockSpec((1, tb, DIL_GROUP_W), lambda b, n: (b, n, 0))
    prev = pl.BlockSpec((1, tb, DIL_GROUP_W), lambda b, n: (b, jnp.maximum(n - 1, 0), 0))
    ng = len(DIL_GROUPS)
    outs = pl.pallas_call(
        functools.partial(_dil_kernel, seq=seq, slopes=_slopes()),
        grid=(batch, seq // tb),
        in_specs=[cur] * ng + [cur] * ng + [prev] * ng + [cur] * ng + [prev] * ng,
        out_specs=[cur] * (2 * ng),
        out_shape=([jax.ShapeDtypeStruct((batch, seq, DIL_GROUP_W), BF16)] * ng
                   + [jax.ShapeDtypeStruct((batch, seq, DIL_GROUP_W), F32)] * ng),
        compiler_params=_params(("arbitrary", "arbitrary")),
        name="dil_attn",
    )(*qs, *ks, *ks, *vs, *vs)
    return outs[:ng], outs[ng:]


def _mix_kernel(x_ref, osb_ref, gate_ref, o0_ref, o1_ref, o2_ref, l0_ref, l1_ref, l2_ref,
                wsb_ref, wdl_ref, wout_ref, g2_ref, wq_ref, sk_ref,
                x1_ref, ht_ref, st_ref, o_scr, l_scr, *, tm):
    def natural(o_ref, l_ref, g):
        r = DIL_GROUPS[g][1]
        if r == 1:
            return o_ref[0].astype(F32), l_ref[0]
        nh = DIL_GROUP_W // LANES
        for c in range(r):
            for hf in range(nh):
                sl = slice(hf * LANES, (hf + 1) * LANES)
                o_scr[hf, pl.ds(c, tm // r, stride=r), :] = o_ref[0, c, :, sl].astype(F32)
                l_scr[hf, pl.ds(c, tm // r, stride=r), :] = l_ref[0, c, :, sl]
        return (jnp.concatenate([o_scr[hf] for hf in range(nh)], axis=1),
                jnp.concatenate([l_scr[hf] for hf in range(nh)], axis=1))

    o0, l0 = natural(o0_ref, l0_ref, 0)
    o1, l1 = natural(o1_ref, l1_ref, 1)
    o2, l2 = natural(o2_ref, l2_ref, 2)
    mx = jnp.maximum(jnp.maximum(l0, l1), l2)
    w0, w1, w2 = jnp.exp(l0 - mx), jnp.exp(l1 - mx), jnp.exp(l2 - mx)
    o_dl = (w0 * o0 + w1 * o1 + w2 * o2) / (w0 + w1 + w2)

    d = x_ref.shape[-1]
    y_sb = jnp.dot(osb_ref[...], wsb_ref[...], preferred_element_type=F32)
    y_dl = jnp.dot(o_dl.astype(BF16), wdl_ref[...], preferred_element_type=F32)
    mixed = gate_ref[:, :d].astype(F32) * y_sb + gate_ref[:, d:].astype(F32) * y_dl
    x1 = x_ref[...] + jnp.dot(mixed.astype(BF16), wout_ref[...], preferred_element_type=F32)
    x1_ref[...] = x1

    ms = jnp.mean(x1 * x1, axis=-1, keepdims=True)
    h = x1 * lax.rsqrt(ms + RMS_EPS) * g2_ref[...]
    ht_ref[...] = h.T.astype(BF16)
    qp = jnp.dot(h.astype(BF16), wq_ref[...], preferred_element_type=F32).astype(BF16)
    for hp in range(2 * PEER_HEADS):
        st_ref[hp] = lax.dot_general(sk_ref[hp], qp[:, hp * PEER_N_KEYS:(hp + 1) * PEER_N_KEYS],
                                     (((1,), (1,)), ((), ())), preferred_element_type=F32)


def _mix(x2, o_sb, gates, o_dl, l_dl, w_sb, w_dl, w_out, g2, w_q, sub_keys, *, batch, seq, tm):
    T, D = x2.shape
    tiles_per_seq = seq // tm
    row = lambda i: (i, 0)

    def group_spec(r):
        if r == 1:
            return pl.BlockSpec((1, tm, DIL_GROUP_W), lambda i: (i // tiles_per_seq, i % tiles_per_seq, 0))
        return pl.BlockSpec((1, r, tm // r, DIL_GROUP_W),
                            lambda i: (i // tiles_per_seq, 0, i % tiles_per_seq, 0))

    def group_view(t, r):
        return t if r == 1 else t.reshape(batch, r, seq // r, DIL_GROUP_W)

    rs = [dil for _, dil in DIL_GROUPS]
    o_dl = [group_view(t, r) for t, r in zip(o_dl, rs)]
    l_dl = [group_view(t, r) for t, r in zip(l_dl, rs)]
    g_specs = [group_spec(r) for r in rs]
    nhp = 2 * PEER_HEADS
    return pl.pallas_call(
        functools.partial(_mix_kernel, tm=tm),
        grid=(T // tm,),
        in_specs=[pl.BlockSpec((tm, D), row), pl.BlockSpec((tm, SB_W), row),
                  pl.BlockSpec((tm, gates.shape[-1]), row)] + g_specs + g_specs
                 + [_const_spec(w.shape) for w in (w_sb, w_dl, w_out, g2, w_q, sub_keys)],
        out_specs=[pl.BlockSpec((tm, D), row), pl.BlockSpec((D, tm), lambda i: (0, i)),
                   pl.BlockSpec((nhp, PEER_N_KEYS, tm), lambda i: (0, 0, i))],
        out_shape=[jax.ShapeDtypeStruct((T, D), F32), jax.ShapeDtypeStruct((D, T), BF16),
                   jax.ShapeDtypeStruct((nhp, PEER_N_KEYS, T), F32)],
        scratch_shapes=[pltpu.VMEM((DIL_GROUP_W // LANES, tm, LANES), F32)] * 2,
        compiler_params=_params(("arbitrary",)),
        name="mix",
    )(x2, o_sb, gates, *o_dl, *l_dl, w_sb, w_dl, w_out, g2, w_q, sub_keys)


def _staircase():
    K = PEER_TOPK
    rows = [(0, b) for b in range(K)]
    for a in range(1, 8):
        rows += [(a, b) for b in range(8)]
    rows += [(a, 0) for a in range(8, K)]
    a = np.array([r[0] for r in rows])
    b = np.array([r[1] for r in rows])
    return a, b, (a + 1) * (b + 1) <= K


def _twice_bf16(x):
    bits = lax.bitcast_convert_type(x.astype(BF16).astype(F32), jnp.uint32)
    return bits | (bits >> 16)


def _topk_kernel(st_ref, flat_ref, rank2_ref, p2_ref, n1_ref, p1_ref, tie_scr, *, tl):
    K = PEER_TOPK
    nk = PEER_N_KEYS
    key_id = lax.broadcasted_iota(jnp.int32, (nk, tl), 0).astype(F32)
    flat = flat_ref[...]
    ok = flat >= 0.0
    neg_inf = -jnp.inf

    def removed(s, valid=None):
        gone = s == neg_inf if valid is None else (s == neg_inf) & valid
        return jnp.sum(jnp.where(gone, 1.0, 0.0), axis=0, keepdims=True)

    def top16_fast(s, want_rank):
        rank = jnp.full((nk, tl), float(K), F32) if want_rank else None
        vals = []
        for k in range(K):
            m = jnp.max(s, axis=0, keepdims=True)
            hit = s == m
            if want_rank:
                rank = jnp.where(hit, float(k), rank)
            s = jnp.where(hit, neg_inf, s)
            vals.append(m)
        return vals, rank, removed(s) != float(K)

    def top16_exact(s, want_rank):
        rank = jnp.full((nk, tl), float(K), F32)
        vals = []
        for k in range(K):
            m = jnp.max(s, axis=0, keepdims=True)
            idx = jnp.min(jnp.where(s == m, key_id, float(nk)), axis=0, keepdims=True)
            hit = key_id == idx
            rank = jnp.where(hit, float(k), rank)
            s = jnp.where(hit, neg_inf, s)
            vals.append(m)
        return vals, rank, None

    def candidates(v1, v2):
        v2lo = jnp.concatenate(v2[:8], axis=0)
        v2hi = jnp.concatenate(v2[8:], axis=0)
        groups = [v1[0] + v2lo, v1[0] + v2hi] + [v1[a] + v2lo for a in range(1, 8)]
        groups.append(jnp.concatenate(v1[8:], axis=0) + v2[0])
        return jnp.where(ok, jnp.concatenate(groups, axis=0), neg_inf)

    def counts_fast(cand, best_max, rank1, s1, v1):
        zsum = jnp.zeros((1, tl), F32)
        for k in range(K):
            m = jnp.max(cand, axis=0, keepdims=True)
            cand = jnp.where(cand == m, neg_inf, cand)
            zsum = zsum + jnp.exp(m - best_max)
        taken = jnp.where((cand == neg_inf) & ok, 1.0, 0.0)
        per_a = [jnp.sum(taken[0:16], axis=0, keepdims=True)]
        per_a += [jnp.sum(taken[8 + 8 * a:16 + 8 * a], axis=0, keepdims=True) for a in range(1, 8)]
        per_a += [taken[72 + j:73 + j] for j in range(8)]
        n1 = jnp.zeros((nk, tl), F32)
        for a in range(K):
            n1 = jnp.where(s1 == v1[a], per_a[a], n1)
        tie = jnp.sum(taken, axis=0, keepdims=True) != float(K)
        return n1, zsum, tie

    def counts_exact(cand, best_max, rank1, s1, v1):
        zsum = jnp.zeros((1, tl), F32)
        n1 = jnp.zeros((nk, tl), F32)
        for k in range(K):
            m = jnp.max(cand, axis=0, keepdims=True)
            f = jnp.min(jnp.where(cand == m, flat, float(K * K)), axis=0, keepdims=True)
            cand = jnp.where(flat == f, neg_inf, cand)
            zsum = zsum + jnp.exp(m - best_max)
            n1 = n1 + jnp.where(rank1 == jnp.floor(f * (1.0 / K)), 1.0, 0.0)
        return n1, zsum, None

    def head(h, top16, counts):
        s1 = st_ref[2 * h]
        s2 = st_ref[2 * h + 1]
        v1, rank1, tie1 = top16(s1, False)
        v2, rank2, tie2 = top16(s2, True)
        best_max = v1[0] + v2[0]
        n1, zsum, tie3 = counts(candidates(v1, v2), best_max, rank1, s1, v1)
        rank2_ref[h] = rank2.astype(BF16)
        p2_ref[h] = jnp.exp(s2 - v2[0]).astype(BF16)
        n1_ref[h] = _twice_bf16(n1)
        p1_ref[h] = _twice_bf16(0.5 * jnp.exp(s1 - v1[0]) / zsum)
        if tie1 is not None:
            tie_scr[h:h + 1, :] = jnp.where(tie1 | tie2 | tie3, 1.0, 0.0)

    for h in range(PEER_HEADS):
        head(h, top16_fast, counts_fast)

    @pl.when(jnp.max(tie_scr[...]) > 0.0)
    def _():
        for h in range(PEER_HEADS):
            @pl.when(jnp.max(tie_scr[h:h + 1, :]) > 0.0)
            def _(h=h):
                head(h, top16_exact, counts_exact)


def _peer_topk(s_t, *, tl):
    nhp, nk, T = s_t.shape
    a, b, okr = _staircase()
    flat = jnp.asarray(np.where(okr, a * PEER_TOPK + b, -1).astype(np.float32)[:, None])
    spec = pl.BlockSpec((PEER_HEADS, nk, tl), lambda i: (0, 0, i))
    return pl.pallas_call(
        functools.partial(_topk_kernel, tl=tl),
        grid=(T // tl,),
        in_specs=[pl.BlockSpec((nhp, nk, tl), lambda i: (0, 0, i)), _const_spec(flat.shape)],
        out_specs=[spec] * 4,
        out_shape=[jax.ShapeDtypeStruct((PEER_HEADS, nk, T), BF16)] * 2
                  + [jax.ShapeDtypeStruct((PEER_HEADS, nk, T), jnp.uint32)] * 2,
        scratch_shapes=[pltpu.VMEM((PEER_HEADS, tl), F32)],
        compiler_params=_params(("arbitrary",)),
        name="peer_topk",
    )(s_t, flat)


def _ffn_kernel(ht_ref, u_ref, vt_ref, rank2_ref, p2_ref, n1_ref, p1_ref, x1_ref,
                out_ref, a_scr, h_scr, acc_scr, *, et, tt, lt, ec):
    e = pl.program_id(1)
    nk = PEER_N_KEYS
    rows = 16

    @pl.when(e == 0)
    def _():
        acc_scr[...] = jnp.zeros_like(acc_scr)

    def packed_row(ref, h, i1_local, ts):
        group = pl.multiple_of(e * (et // nk) + (i1_local // 8) * 8, 8)
        words = ref[h, pl.ds(group, 8), ts]
        word = jnp.broadcast_to(words[i1_local % 8:i1_local % 8 + 1, :], (rows // 2, lt))
        return jnp.concatenate([pltpu.bitcast(word, BF16)] * (nk // rows), axis=0)

    ahead = a_scr.shape[0] - 1

    def expert_scores(c):
        if c < et // ec:
            a_scr[c % (ahead + 1)] = jnp.dot(u_ref[c * ec:(c + 1) * ec, :], ht_ref[...],
                                             preferred_element_type=F32)

    for c in range(ahead):
        expert_scores(c)
    group = h_scr.shape[1] // ec
    for c in range(et // ec):
        grp = c // group
        expert_scores(c + ahead)
        for j in range(ec // nk):
            i1 = c * (ec // nk) + j
            rs = slice(j * nk, (j + 1) * nk)
            for t0 in range(0, tt, lt):
                ts = slice(t0, t0 + lt)
                half_gate = jnp.zeros((nk, lt), BF16)
                for h in range(PEER_HEADS):
                    cnt = packed_row(n1_ref, h, i1, ts)
                    p1 = packed_row(p1_ref, h, i1, ts)
                    sel = jnp.where(rank2_ref[h, :, ts] < cnt, p2_ref[h, :, ts], jnp.zeros((nk, lt), BF16))
                    half_gate = half_gate + sel * p1
                a = a_scr[c % (ahead + 1), rs, ts]
                twice_gelu = a + a * lax.erf(a * math.sqrt(0.5))
                h_scr[grp % 2, (c % group) * ec + j * nk:(c % group) * ec + (j + 1) * nk, ts] = (
                    twice_gelu.astype(BF16) * half_gate)
        if (c + 1) % group == 0:
            acc_scr[...] += jnp.dot(vt_ref[:, grp * group * ec:(grp + 1) * group * ec], h_scr[grp % 2],
                                    preferred_element_type=F32)

    @pl.when(e == pl.num_programs(1) - 1)
    def _():
        out_ref[...] = x1_ref[...] + acc_scr[...].T


def _peer_ffn(h_t, u, v_t, rank2, p2, n1, p1, x1, *, tt, et, lt, ec):
    D, T = h_t.shape
    E = u.shape[0]
    nk = PEER_N_KEYS
    tab = pl.BlockSpec((PEER_HEADS, nk, tt), lambda t, e: (0, 0, t))
    return pl.pallas_call(
        functools.partial(_ffn_kernel, et=et, tt=tt, lt=lt, ec=ec),
        grid=(T // tt, E // et),
        in_specs=[pl.BlockSpec((D, tt), lambda t, e: (0, t)),
                  pl.BlockSpec((et, D), lambda t, e: (e, 0)),
                  pl.BlockSpec((D, et), lambda t, e: (0, e)),
                  tab, tab, tab, tab,
                  pl.BlockSpec((tt, D), lambda t, e: (t, 0))],
        out_specs=pl.BlockSpec((tt, D), lambda t, e: (t, 0)),
        out_shape=jax.ShapeDtypeStruct((T, D), F32),
        scratch_shapes=[pltpu.VMEM((et // ec, ec, tt), F32), pltpu.VMEM((2, 4 * ec, tt), BF16),
                        pltpu.VMEM((D, tt), F32)],
        compiler_params=_params(("arbitrary", "arbitrary")),
        name="peer_ffn",
    )(h_t, u, v_t, rank2, p2, n1, p1, x1)


def _tiles(batch, seq):
    coarsest = DIL_GROUPS[-1][1] * DIL_BLOCK
    assert seq % coarsest == 0, "sequence must hold whole blocks of the most dilated group"
    tokens = batch * seq
    tm = min(512, seq)
    return dict(tm_in=tm, tm_mix=min(256, seq), tq=256, sb_unroll=2, tl=128,
                tt=min(512, tokens), et=2048, ec=256, lt=256)


def _suffix_ones_matrix():
    s = np.arange(LANES)
    tri = (s[:, None] > s[None, :]).astype(np.float32)
    half = np.concatenate([tri, np.ones((LANES, LANES), np.float32)], axis=1)
    return jnp.asarray(np.concatenate([half, half], axis=0), BF16)


def _block_diag_ones():
    h = np.arange(DIL_W) // HEAD_DIM
    return jnp.asarray((h[:, None] == h[None, :]).astype(np.float32), BF16)


def kernel(x, norm1_gain, w_in, b_gate, q_norm_gain, k_norm_gain, w_sb_out, w_dil_out, w_out,
           norm2_gain, w_peer_q, peer_sub_keys, peer_u, peer_v):
    B, S, D = x.shape
    depth = w_in.shape[0]
    t = _tiles(B, S)
    mj = _suffix_ones_matrix()
    ones_bd = _block_diag_ones()
    x2 = x.reshape(B * S, D)
    for l in range(depth):
        outs = _in_proj(
            x2, norm1_gain[l][None, :], w_in[l].astype(BF16), b_gate[l][None, :],
            q_norm_gain[l].reshape(1, DIL_W), k_norm_gain[l].reshape(1, DIL_W), ones_bd,
            batch=B, seq=S, tm=t["tm_in"])
        q_sb, k_sb, v_sb = outs[0:3]
        q_dl, k_dl, v_dl, gates = outs[3:6], outs[6:9], outs[9:12], outs[12]
        o_sb = _sb_attention(q_sb, k_sb, v_sb, mj, batch=B, seq=S, tq=t["tq"], unroll=t["sb_unroll"])
        o_dl, l_dl = _dil_attention(q_dl, k_dl, v_dl, batch=B, seq=S)
        nhp = 2 * PEER_HEADS
        x1, h_t, s_t = _mix(
            x2, o_sb, gates, o_dl, l_dl, w_sb_out[l].astype(BF16), w_dil_out[l].astype(BF16),
            w_out[l].astype(BF16), norm2_gain[l][None, :], w_peer_q[l].astype(BF16),
            peer_sub_keys[l].reshape(nhp, PEER_N_KEYS, -1).astype(BF16),
            batch=B, seq=S, tm=t["tm_mix"])
        rank2, p2, n1, p1 = _peer_topk(s_t, tl=t["tl"])
        x2 = _peer_ffn(h_t, peer_u[l].astype(BF16), peer_v[l].T.astype(BF16), rank2, p2, n1, p1, x1,
                       tt=t["tt"], et=t["et"], lt=t["lt"], ec=t["ec"])
    return x2.reshape(B, S, D)
```

```python
import functools
import math

import numpy as np
import jax
import jax.numpy as jnp
from jax import lax
from jax.experimental import pallas as pl
from jax.experimental.pallas import tpu as pltpu

F32 = jnp.float32
BF16 = jnp.bfloat16

HEAD_DIM = 64
SB_HEADS = 8
DIL_GROUPS = ((128, 1), (512, 4), (2048, 16))
DIL_HEADS_PER_GROUP = 4
DIL_HEADS = DIL_HEADS_PER_GROUP * len(DIL_GROUPS)
PEER_HEADS = 8
PEER_N_KEYS = 128
PEER_TOPK = 16
RMS_EPS = 1e-6
ALIBI_MAX_EXP = 8.0
SB_W = SB_HEADS * HEAD_DIM
DIL_W = DIL_HEADS * HEAD_DIM
DIL_GROUP_W = DIL_HEADS_PER_GROUP * HEAD_DIM
QK_SCALE = HEAD_DIM ** -0.5
NEG_BIG = -1e30

LANES = 128
DIL_BLOCK = 128
VMEM_LIMIT = 56 * 1024 * 1024


def _slopes():
    h = np.arange(1, DIL_HEADS + 1, dtype=np.float32)
    return np.float32(2.0) ** (np.float32(-ALIBI_MAX_EXP) * h / np.float32(DIL_HEADS))


def _params(sem, flags=None):
    return pltpu.CompilerParams(dimension_semantics=sem, vmem_limit_bytes=VMEM_LIMIT, flags=flags)


def _const_spec(shape):
    return pl.BlockSpec(shape, lambda *_: (0,) * len(shape))


def _in_proj_kernel(x_ref, g1_ref, w_ref, bg_ref, gq_ref, gk_ref, ones_ref,
                    qsb_ref, ksb_ref, vsb_ref,
                    qd0_ref, qd1_ref, qd2_ref, kd0_ref, kd1_ref, kd2_ref,
                    vd0_ref, vd1_ref, vd2_ref, gate_ref, scr_ref, *, tm):
    x = x_ref[...]
    ms = jnp.mean(x * x, axis=-1, keepdims=True)
    xn = (x * lax.rsqrt(ms + RMS_EPS) * g1_ref[...]).astype(BF16)

    def proj(lo, width):
        return jnp.dot(xn, w_ref[:, lo:lo + width], preferred_element_type=F32)

    qsb_ref[...] = (proj(0, SB_W) * (QK_SCALE * math.log2(math.e))).astype(BF16)
    ksb_ref[...] = proj(SB_W, SB_W).astype(BF16)
    vsb_ref[...] = proj(2 * SB_W, SB_W).astype(BF16)

    def head_norm(t, gain):
        ssq = jnp.dot((t * t).astype(BF16), ones_ref[...], preferred_element_type=F32)
        return t * lax.rsqrt(ssq * (1.0 / HEAD_DIM) + RMS_EPS) * gain

    def split_groups(t, outs):
        outs[0][0] = t[:, :DIL_GROUP_W].astype(BF16)
        for g in (1, 2):
            r = DIL_GROUPS[g][1]
            for hf in range(DIL_GROUP_W // LANES):
                lo = g * DIL_GROUP_W + hf * LANES
                scr_ref[hf] = t[:, lo:lo + LANES]
            for c in range(r):
                for hf in range(DIL_GROUP_W // LANES):
                    rows = scr_ref[hf, pl.ds(c, tm // r, stride=r), :]
                    outs[g][0, c, :, hf * LANES:(hf + 1) * LANES] = rows.astype(BF16)

    base = 3 * SB_W
    q = head_norm(proj(base, DIL_W), gq_ref[...]) * QK_SCALE
    split_groups(q, (qd0_ref, qd1_ref, qd2_ref))
    k = head_norm(proj(base + DIL_W, DIL_W), gk_ref[...])
    split_groups(k, (kd0_ref, kd1_ref, kd2_ref))
    split_groups(proj(base + 2 * DIL_W, DIL_W), (vd0_ref, vd1_ref, vd2_ref))

    gw = gate_ref.shape[-1]
    gate_ref[...] = jax.nn.sigmoid(proj(base + 3 * DIL_W, gw) + bg_ref[...]).astype(BF16)


def _in_proj(x2, g1, w_in, b_gate, gq, gk, ones_bd, *, batch, seq, tm):
    T, D = x2.shape
    tiles_per_seq = seq // tm
    gw = b_gate.shape[-1]
    row = lambda i: (i, 0)
    nat = pl.BlockSpec((1, tm, DIL_GROUP_W), lambda i: (i // tiles_per_seq, i % tiles_per_seq, 0))

    def perm_spec(r):
        return pl.BlockSpec((1, r, tm // r, DIL_GROUP_W),
                            lambda i: (i // tiles_per_seq, 0, i % tiles_per_seq, 0))

    def perm_shape(r):
        return jax.ShapeDtypeStruct((batch, r, seq // r, DIL_GROUP_W), BF16)

    r1, r2 = DIL_GROUPS[1][1], DIL_GROUPS[2][1]
    dil_shapes = [jax.ShapeDtypeStruct((batch, seq, DIL_GROUP_W), BF16), perm_shape(r1), perm_shape(r2)]
    dil_specs = [nat, perm_spec(r1), perm_spec(r2)]
    out_shape = ([jax.ShapeDtypeStruct((T, SB_W), BF16)] * 3 + dil_shapes * 3
                 + [jax.ShapeDtypeStruct((T, gw), BF16)])
    out_specs = ([pl.BlockSpec((tm, SB_W), row)] * 3 + dil_specs * 3 + [pl.BlockSpec((tm, gw), row)])
    return pl.pallas_call(
        functools.partial(_in_proj_kernel, tm=tm),
        grid=(T // tm,),
        in_specs=[pl.BlockSpec((tm, D), row), _const_spec(g1.shape), _const_spec(w_in.shape),
                  _const_spec(b_gate.shape), _const_spec(gq.shape), _const_spec(gk.shape),
                  _const_spec(ones_bd.shape)],
        out_specs=out_specs,
        out_shape=out_shape,
        scratch_shapes=[pltpu.VMEM((DIL_GROUP_W // LANES, tm, LANES), F32)],
        compiler_params=_params(("arbitrary",)),
        name="in_proj",
    )(x2, g1, w_in, b_gate, gq, gk, ones_bd)


def _sb_kernel(q_ref, k_ref, v_ref, mj_ref, o_ref, q_scr, c_scr, acc_scr, *, tq, unroll):
    qi = pl.program_id(2)
    lane = lax.broadcasted_iota(jnp.int32, (tq, LANES), 1)
    q = q_ref[0]
    nh = tq // LANES
    sign = jnp.uint32(0x80000000)
    for hh in range(2):
        in_head = (lane >= hh * HEAD_DIM) & (lane < (hh + 1) * HEAD_DIM)
        q_scr[hh] = jnp.where(in_head, q, jnp.zeros_like(q))

    def trip(first_kb, count, diagonal):
        chains = [(hh, u) for u in range(count) for hh in range(2)]
        starts = [pl.multiple_of((first_kb - u) * tq, tq) for u in range(count)]
        carried = [c_scr[0], c_scr[1]]
        accs = [acc_scr[0], acc_scr[1]]
        live = [dict() for _ in chains]

        def qk(i):
            hh, u = chains[i]
            live[i]["z"] = lax.dot_general(q_scr[hh], k_ref[0, pl.ds(starts[u], tq), :],
                                           (((1,), (1,)), ((), ())), preferred_element_type=F32)

        def pointwise(i):
            z = live[i].pop("z")
            neg_abs = lax.bitcast_convert_type(lax.bitcast_convert_type(z, jnp.uint32) | sign, F32)
            t = jnp.maximum(z, 0.0) + jnp.log(1.0 + jnp.exp2(neg_abs)) * math.log2(math.e)
            lb = z - t
            if diagonal:
                causal = (lax.broadcasted_iota(jnp.int32, (tq, tq), 1)
                          < lax.broadcasted_iota(jnp.int32, (tq, tq), 0))
                t = jnp.where(causal, t, 0.0)
                lb = jnp.where(causal, lb, NEG_BIG)
            live[i].update(lb=lb, t=t.astype(BF16),
                           tot=[jnp.sum(t[:, hf * LANES:(hf + 1) * LANES], axis=1, keepdims=True)
                                for hf in range(nh)])

        def suffix(i):
            live[i]["r"] = jnp.dot(live[i].pop("t"), mj_ref[...], preferred_element_type=F32)

        def weigh(i):
            hh, _ = chains[i]
            lb, r, tot = live[i].pop("lb"), live[i].pop("r"), live[i].pop("tot")
            parts = [None] * nh
            for hf in reversed(range(nh)):
                sl = slice(hf * LANES, (hf + 1) * LANES)
                parts[hf] = jnp.exp2(lb[:, sl] - carried[hh] - r[:, sl]).astype(BF16)
                carried[hh] = carried[hh] + tot[hf]
            live[i]["a"] = jnp.concatenate(parts, axis=1)

        def pv(i):
            hh, u = chains[i]
            accs[hh] = accs[hh] + jnp.dot(live[i].pop("a"), v_ref[0, pl.ds(starts[u], tq), :],
                                          preferred_element_type=F32)

        stages = (qk, pointwise, suffix, weigh, pv)
        for step in range(len(chains) + len(stages) - 1):
            for s in reversed(range(len(stages))):
                if 0 <= step - s < len(chains):
                    stages[s](step - s)
        c_scr[0], c_scr[1] = carried
        acc_scr[0], acc_scr[1] = accs

    c_scr[...] = jnp.zeros_like(c_scr)
    acc_scr[...] = jnp.zeros_like(acc_scr)
    trip(qi, 1, True)
    groups = qi // unroll

    def grouped(it, carry):
        trip(qi - 1 - it * unroll, unroll, False)
        return carry

    lax.fori_loop(0, groups, grouped, 0)
    rest = qi - groups * unroll
    size = unroll // 2
    while size >= 1:
        @pl.when((rest & size) != 0)
        def _(size=size):
            trip((rest & (2 * size - 1)) - 1, size, False)
        size //= 2
    o_ref[0] =jnp.where(lane < HEAD_DIM, acc_scr[0], acc_scr[1]).astype(BF16)


def _sb_attention(q, k, v, mj, *, batch, seq, tq, unroll):
    q3, k3, v3 = (t.reshape(batch, seq, SB_W) for t in (q, k, v))
    pairs = SB_W // LANES
    out = pl.pallas_call(
        functools.partial(_sb_kernel, tq=tq, unroll=unroll),
        grid=(batch, pairs, seq // tq),
        in_specs=[pl.BlockSpec((1, tq, LANES), lambda b, p, i: (b, i, p)),
                  pl.BlockSpec((1, seq, LANES), lambda b, p, i: (b, 0, p)),
                  pl.BlockSpec((1, seq, LANES), lambda b, p, i: (b, 0, p)),
                  _const_spec(mj.shape)],
        out_specs=pl.BlockSpec((1, tq, LANES), lambda b, p, i: (b, i, p)),
        out_shape=jax.ShapeDtypeStruct((batch, seq, SB_W), BF16),
        scratch_shapes=[pltpu.VMEM((2, tq, LANES), BF16), pltpu.VMEM((2, tq, LANES), F32),
                        pltpu.VMEM((2, tq, LANES), F32)],
        compiler_params=_params(("arbitrary", "arbitrary", "arbitrary")),
        name="sb_attn",
    )(q3, k3, v3, mj)
    return out.reshape(batch * seq, SB_W)


def _dil_kernel(*refs, seq, slopes):
    n = pl.program_id(1)
    ng = len(DIL_GROUPS)
    q_refs, kc_refs, kp_refs, vc_refs, vp_refs = (refs[i * ng:(i + 1) * ng] for i in range(5))
    o_refs, l_refs = refs[5 * ng:6 * ng], refs[6 * ng:7 * ng]
    tb = DIL_BLOCK
    lane = lax.broadcasted_iota(jnp.int32, (tb, LANES), 1)
    row = lax.broadcasted_iota(jnp.int32, (tb, tb), 0)
    col = lax.broadcasted_iota(jnp.int32, (tb, tb), 1)
    d_cur = row - col
    d_prev = d_cur + tb
    off_cur = d_cur.astype(F32)
    off_prev = d_prev.astype(F32)
    contract = (((1,), (1,)), ((), ()))
    valid_cur = d_cur >= 0
    heads = [(g, jp, hh) for g in range(ng) for jp in range(DIL_GROUP_W // LANES) for hh in range(2)]
    lanes_of = lambda jp: slice(jp * LANES, (jp + 1) * LANES)

    scores = []
    for g, jp, hh in heads:
        sl = lanes_of(jp)
        in_head = (lane >= hh * HEAD_DIM) & (lane < (hh + 1) * HEAD_DIM)
        q = q_refs[g][0, :, sl]
        qm = jnp.where(in_head, q, jnp.zeros_like(q))
        scores.append((lax.dot_general(qm, kc_refs[g][0, :, sl], contract, preferred_element_type=F32),
                       lax.dot_general(qm, kp_refs[g][0, :, sl], contract, preferred_element_type=F32)))

    masked = []
    for (g, jp, hh), (zc, zp) in zip(heads, scores):
        window, dil = DIL_GROUPS[g]
        has_prev = (n % (seq // dil // tb)) != 0
        valid_prev = (d_prev <= window // dil) & has_prev
        slope = float(slopes[g * DIL_HEADS_PER_GROUP + 2 * jp + hh]) * dil
        zc = jnp.where(valid_cur, zc - slope * off_cur, NEG_BIG)
        zp = jnp.where(valid_prev, zp - slope * off_prev, NEG_BIG)
        m = jnp.max(jnp.maximum(zc, zp), axis=1, keepdims=True)
        masked.append((zc, zp, m))

    weighted = []
    for (g, jp, hh), (zc, zp, m) in zip(heads, masked):
        sl = lanes_of(jp)
        pc = jnp.exp(zc - m)
        pp = jnp.exp(zp - m)
        den = jnp.sum(pc + pp, axis=1, keepdims=True)
        acc = (jnp.dot(pc.astype(BF16), vc_refs[g][0, :, sl], preferred_element_type=F32)
               + jnp.dot(pp.astype(BF16), vp_refs[g][0, :, sl], preferred_element_type=F32))
        weighted.append((acc, den, m))

    first = lane < HEAD_DIM
    for i in range(0, len(heads), 2):
        g, jp, _ = heads[i]
        sl = lanes_of(jp)
        (a0, d0, m0), (a1, d1, m1) = weighted[i], weighted[i + 1]
        o_refs[g][0, :, sl] = jnp.where(first, a0 / d0, a1 / d1).astype(BF16)
        l_refs[g][0, :, sl] = jnp.where(first, jnp.broadcast_to(m0 + jnp.log(d0), (tb, LANES)),
                                        jnp.broadcast_to(m1 + jnp.log(d1), (tb, LANES)))


def _dil_attention(qs, ks, vs, *, batch, seq):
    tb = DIL_BLOCK
    flat = lambda t: t.reshape(batch, seq, DIL_GROUP_W)
    qs, ks, vs = ([flat(t) for t in ts] for ts in (qs, ks, vs))
    cur = pl.BlockSpec((1, tb, DIL_GROUP_W), lambda b, n: (b, n, 0))
    prev = pl.BlockSpec((1, tb, DIL_GROUP_W), lambda b, n: (b, jnp.maximum(n - 1, 0), 0))
    ng = len(DIL_GROUPS)
    outs = pl.pallas_call(
        functools.partial(_dil_kernel, seq=seq, slopes=_slopes()),
        grid=(batch, seq // tb),
        in_specs=[cur] * ng + [cur] * ng + [prev] * ng + [cur] * ng + [prev] * ng,
        out_specs=[cur] * (2 * ng),
        out_shape=([jax.ShapeDtypeStruct((batch, seq, DIL_GROUP_W), BF16)] * ng
                   + [jax.ShapeDtypeStruct((batch, seq, DIL_GROUP_W), F32)] * ng),
        compiler_params=_params(("arbitrary", "arbitrary")),
        name="dil_attn",
    )(*qs, *ks, *ks, *vs, *vs)
    return outs[:ng], outs[ng:]


def _mix_kernel(x_ref, osb_ref, gate_ref, o0_ref, o1_ref, o2_ref, l0_ref, l1_ref, l2_ref,
                wsb_ref, wdl_ref, wout_ref, g2_ref, wq_ref, sk_ref,
                x1_ref, ht_ref, st_ref, o_scr, l_scr, *, tm):
    def natural(o_ref, l_ref, g):
        r = DIL_GROUPS[g][1]
        if r == 1:
            return o_ref[0].astype(F32), l_ref[0]
        nh = DIL_GROUP_W // LANES
        for c in range(r):
            for hf in range(nh):
                sl = slice(hf * LANES, (hf + 1) * LANES)
                o_scr[hf, pl.ds(c, tm // r, stride=r), :] = o_ref[0, c, :, sl].astype(F32)
                l_scr[hf, pl.ds(c, tm // r, stride=r), :] = l_ref[0, c, :, sl]
        return (jnp.concatenate([o_scr[hf] for hf in range(nh)], axis=1),
                jnp.concatenate([l_scr[hf] for hf in range(nh)], axis=1))

    o0, l0 = natural(o0_ref, l0_ref, 0)
    o1, l1 = natural(o1_ref, l1_ref, 1)
    o2, l2 = natural(o2_ref, l2_ref, 2)
    mx = jnp.maximum(jnp.maximum(l0, l1), l2)
    w0, w1, w2 = jnp.exp(l0 - mx), jnp.exp(l1 - mx), jnp.exp(l2 - mx)
    o_dl = (w0 * o0 + w1 * o1 + w2 * o2) / (w0 + w1 + w2)

    d = x_ref.shape[-1]
    y_sb = jnp.dot(osb_ref[...], wsb_ref[...], preferred_element_type=F32)
    y_dl = jnp.dot(o_dl.astype(BF16), wdl_ref[...], preferred_element_type=F32)
    mixed = gate_ref[:, :d].astype(F32) * y_sb + gate_ref[:, d:].astype(F32) * y_dl
    x1 = x_ref[...] + jnp.dot(mixed.astype(BF16), wout_ref[...], preferred_element_type=F32)
    x1_ref[...] = x1

    ms = jnp.mean(x1 * x1, axis=-1, keepdims=True)
    h = x1 * lax.rsqrt(ms + RMS_EPS) * g2_ref[...]
    ht_ref[...] = h.T.astype(BF16)
    qp = jnp.dot(h.astype(BF16), wq_ref[...], preferred_element_type=F32).astype(BF16)
    for hp in range(2 * PEER_HEADS):
        st_ref[hp] = lax.dot_general(sk_ref[hp], qp[:, hp * PEER_N_KEYS:(hp + 1) * PEER_N_KEYS],
                                     (((1,), (1,)), ((), ())), preferred_element_type=F32)


def _mix(x2, o_sb, gates, o_dl, l_dl, w_sb, w_dl, w_out, g2, w_q, sub_keys, *, batch, seq, tm):
    T, D = x2.shape
    tiles_per_seq = seq // tm
    row = lambda i: (i, 0)

    def group_spec(r):
        if r == 1:
            return pl.BlockSpec((1, tm, DIL_GROUP_W), lambda i: (i // tiles_per_seq, i % tiles_per_seq, 0))
        return pl.BlockSpec((1, r, tm // r, DIL_GROUP_W),
                            lambda i: (i // tiles_per_seq, 0, i % tiles_per_seq, 0))

    def group_view(t, r):
        return t if r == 1 else t.reshape(batch, r, seq // r, DIL_GROUP_W)

    rs = [dil for _, dil in DIL_GROUPS]
    o_dl = [group_view(t, r) for t, r in zip(o_dl, rs)]
    l_dl = [group_view(t, r) for t, r in zip(l_dl, rs)]
    g_specs = [group_spec(r) for r in rs]
    nhp = 2 * PEER_HEADS
    return pl.pallas_call(
        functools.partial(_mix_kernel, tm=tm),
        grid=(T // tm,),
        in_specs=[pl.BlockSpec((tm, D), row), pl.BlockSpec((tm, SB_W), row),
                  pl.BlockSpec((tm, gates.shape[-1]), row)] + g_specs + g_specs
                 + [_const_spec(w.shape) for w in (w_sb, w_dl, w_out, g2, w_q, sub_keys)],
        out_specs=[pl.BlockSpec((tm, D), row), pl.BlockSpec((D, tm), lambda i: (0, i)),
                   pl.BlockSpec((nhp, PEER_N_KEYS, tm), lambda i: (0, 0, i))],
        out_shape=[jax.ShapeDtypeStruct((T, D), F32), jax.ShapeDtypeStruct((D, T), BF16),
                   jax.ShapeDtypeStruct((nhp, PEER_N_KEYS, T), F32)],
        scratch_shapes=[pltpu.VMEM((DIL_GROUP_W // LANES, tm, LANES), F32)] * 2,
        compiler_params=_params(("arbitrary",)),
        name="mix",
    )(x2, o_sb, gates, *o_dl, *l_dl, w_sb, w_dl, w_out, g2, w_q, sub_keys)


def _staircase():
    K = PEER_TOPK
    rows = [(0, b) for b in range(K)]
    for a in range(1, 8):
        rows += [(a, b) for b in range(8)]
    rows += [(a, 0) for a in range(8, K)]
    a = np.array([r[0] for r in rows])
    b = np.array([r[1] for r in rows])
    return a, b, (a + 1) * (b + 1) <= K


def _twice_bf16(x):
    bits = lax.bitcast_convert_type(x.astype(BF16).astype(F32), jnp.uint32)
    return bits | (bits >> 16)


def _topk_kernel(st_ref, flat_ref, rank2_ref, p2_ref, n1_ref, p1_ref, tie_scr, *, tl):
    K = PEER_TOPK
    nk = PEER_N_KEYS
    key_id = lax.broadcasted_iota(jnp.int32, (nk, tl), 0).astype(F32)
    flat = flat_ref[...]
    ok = flat >= 0.0
    neg_inf = -jnp.inf

    def removed(s, valid=None):
        gone = s == neg_inf if valid is None else (s == neg_inf) & valid
        return jnp.sum(jnp.where(gone, 1.0, 0.0), axis=0, keepdims=True)

    def top16_fast(s1, s2):
        state = [[s1, None, []], [s2, jnp.full((nk, tl), float(K), F32), []]]
        for k in range(K):
            for st in state:
                m = jnp.max(st[0], axis=0, keepdims=True)
                hit = st[0] == m
                if st[1] is not None:
                    st[1] = jnp.where(hit, float(k), st[1])
                st[0] = jnp.where(hit, neg_inf, st[0])
                st[2].append(m)
        return [(vals, rank, removed(s) != float(K)) for s, rank, vals in state]

    def top16_exact(s1, s2):
        out = []
        for s in (s1, s2):
            rank = jnp.full((nk, tl), float(K), F32)
            vals = []
            for k in range(K):
                m = jnp.max(s, axis=0, keepdims=True)
                idx = jnp.min(jnp.where(s == m, key_id, float(nk)), axis=0, keepdims=True)
                hit = key_id == idx
                rank = jnp.where(hit, float(k), rank)
                s = jnp.where(hit, neg_inf, s)
                vals.append(m)
            out.append((vals, rank, None))
        return out

    def candidates(v1, v2):
        v2lo = jnp.concatenate(v2[:8], axis=0)
        v2hi = jnp.concatenate(v2[8:], axis=0)
        groups = [v1[0] + v2lo, v1[0] + v2hi] + [v1[a] + v2lo for a in range(1, 8)]
        groups.append(jnp.concatenate(v1[8:], axis=0) + v2[0])
        return jnp.where(ok, jnp.concatenate(groups, axis=0), neg_inf)

    def counts_fast(cand, best_max, rank1, s1, v1):
        zsum = jnp.zeros((1, tl), F32)
        for k in range(K):
            m = jnp.max(cand, axis=0, keepdims=True)
            cand = jnp.where(cand == m, neg_inf, cand)
            zsum = zsum + jnp.exp(m - best_max)
        taken = jnp.where((cand == neg_inf) & ok, 1.0, 0.0)
        per_a = [jnp.sum(taken[0:16], axis=0, keepdims=True)]
        per_a += [jnp.sum(taken[8 + 8 * a:16 + 8 * a], axis=0, keepdims=True) for a in range(1, 8)]
        per_a += [taken[72 + j:73 + j] for j in range(8)]
        n1 = jnp.zeros((nk, tl), F32)
        for a in range(K):
            n1 = jnp.where(s1 == v1[a], per_a[a], n1)
        tie = jnp.sum(taken, axis=0, keepdims=True) != float(K)
        return n1, zsum, tie

    def counts_exact(cand, best_max, rank1, s1, v1):
        zsum = jnp.zeros((1, tl), F32)
        n1 = jnp.zeros((nk, tl), F32)
        for k in range(K):
            m = jnp.max(cand, axis=0, keepdims=True)
            f = jnp.min(jnp.where(cand == m, flat, float(K * K)), axis=0, keepdims=True)
            cand = jnp.where(flat == f, neg_inf, cand)
            zsum = zsum + jnp.exp(m - best_max)
            n1 = n1 + jnp.where(rank1 == jnp.floor(f * (1.0 / K)), 1.0, 0.0)
        return n1, zsum, None

    def head(h, top16, counts):
        s1 = st_ref[2 * h]
        s2 = st_ref[2 * h + 1]
        (v1, rank1, tie1), (v2, rank2, tie2) = top16(s1, s2)
        best_max = v1[0] + v2[0]
        n1, zsum, tie3 = counts(candidates(v1, v2), best_max, rank1, s1, v1)
        rank2_ref[h] = rank2.astype(BF16)
        p2_ref[h] = jnp.exp(s2 - v2[0]).astype(BF16)
        n1_ref[h] = _twice_bf16(n1)
        p1_ref[h] = _twice_bf16(0.5 * jnp.exp(s1 - v1[0]) / zsum)
        if tie1 is not None:
            tie_scr[h:h + 1, :] = jnp.where(tie1 | tie2 | tie3, 1.0, 0.0)

    for h in range(PEER_HEADS):
        head(h, top16_fast, counts_fast)

    @pl.when(jnp.max(tie_scr[...]) > 0.0)
    def _():
        for h in range(PEER_HEADS):
            @pl.when(jnp.max(tie_scr[h:h + 1, :]) > 0.0)
            def _(h=h):
                head(h, top16_exact, counts_exact)


def _peer_topk(s_t, *, tl):
    nhp, nk, T = s_t.shape
    a, b, okr = _staircase()
    flat = jnp.asarray(np.where(okr, a * PEER_TOPK + b, -1).astype(np.float32)[:, None])
    spec = pl.BlockSpec((PEER_HEADS, nk, tl), lambda i: (0, 0, i))
    return pl.pallas_call(
        functools.partial(_topk_kernel, tl=tl),
        grid=(T // tl,),
        in_specs=[pl.BlockSpec((nhp, nk, tl), lambda i: (0, 0, i)), _const_spec(flat.shape)],
        out_specs=[spec] * 4,
        out_shape=[jax.ShapeDtypeStruct((PEER_HEADS, nk, T), BF16)] * 2
                  + [jax.ShapeDtypeStruct((PEER_HEADS, nk, T), jnp.uint32)] * 2,
        scratch_shapes=[pltpu.VMEM((PEER_HEADS, tl), F32)],
        compiler_params=_params(("arbitrary",)),
        name="peer_topk",
    )(s_t, flat)


def _ffn_kernel(ht_ref, u_ref, vt_ref, rank2_ref, p2_ref, n1_ref, p1_ref, x1_ref,
                out_ref, a_scr, h_scr, acc_scr, *, et, tt, lt, ec):
    e = pl.program_id(1)
    nk = PEER_N_KEYS
    rows = 16

    @pl.when(e == 0)
    def _():
        acc_scr[...] = jnp.zeros_like(acc_scr)

    def packed_row(ref, h, i1_local, ts):
        group = pl.multiple_of(e * (et // nk) + (i1_local // 8) * 8, 8)
        words = ref[h, pl.ds(group, 8), ts]
        word = jnp.broadcast_to(words[i1_local % 8:i1_local % 8 + 1, :], (rows // 2, lt))
        return jnp.concatenate([pltpu.bitcast(word, BF16)] * (nk // rows), axis=0)

    ahead = a_scr.shape[0] - 1

    def expert_scores(c):
        if c < et // ec:
            a_scr[c % (ahead + 1)] = jnp.dot(u_ref[c * ec:(c + 1) * ec, :], ht_ref[...],
                                             preferred_element_type=F32)

    for c in range(ahead):
        expert_scores(c)
    group = h_scr.shape[1] // ec
    for c in range(et // ec):
        grp = c // group
        expert_scores(c + ahead)
        for j in range(ec // nk):
            i1 = c * (ec // nk) + j
            rs = slice(j * nk, (j + 1) * nk)
            for t0 in range(0, tt, lt):
                ts = slice(t0, t0 + lt)
                half_gate = jnp.zeros((nk, lt), BF16)
                for h in range(PEER_HEADS):
                    cnt = packed_row(n1_ref, h, i1, ts)
                    p1 = packed_row(p1_ref, h, i1, ts)
                    sel = jnp.where(rank2_ref[h, :, ts] < cnt, p2_ref[h, :, ts], jnp.zeros((nk, lt), BF16))
                    half_gate = half_gate + sel * p1
                a = a_scr[c % (ahead + 1), rs, ts]
                twice_gelu = a + a * lax.erf(a * math.sqrt(0.5))
                h_scr[grp % 2, (c % group) * ec + j * nk:(c % group) * ec + (j + 1) * nk, ts] = (
                    twice_gelu.astype(BF16) * half_gate)
        if (c + 1) % group == 0:
            acc_scr[...] += jnp.dot(vt_ref[:, grp * group * ec:(grp + 1) * group * ec], h_scr[grp % 2],
                                    preferred_element_type=F32)

    @pl.when(e == pl.num_programs(1) - 1)
    def _():
        out_ref[...] = x1_ref[...] + acc_scr[...].T


def _peer_ffn(h_t, u, v_t, rank2, p2, n1, p1, x1, *, tt, et, lt, ec):
    D, T = h_t.shape
    E = u.shape[0]
    nk = PEER_N_KEYS
    tab = pl.BlockSpec((PEER_HEADS, nk, tt), lambda t, e: (0, 0, t))
    return pl.pallas_call(
        functools.partial(_ffn_kernel, et=et, tt=tt, lt=lt, ec=ec),
        grid=(T // tt, E // et),
        in_specs=[pl.BlockSpec((D, tt), lambda t, e: (0, t)),
                  pl.BlockSpec((et, D), lambda t, e: (e, 0)),
                  pl.BlockSpec((D, et), lambda t, e: (0, e)),
                  tab, tab, tab, tab,
                  pl.BlockSpec((tt, D), lambda t, e: (t, 0))],
        out_specs=pl.BlockSpec((tt, D), lambda t, e: (t, 0)),
        out_shape=jax.ShapeDtypeStruct((T, D), F32),
        scratch_shapes=[pltpu.VMEM((et // ec, ec, tt), F32), pltpu.VMEM((2, 4 * ec, tt), BF16),
                        pltpu.VMEM((D, tt), F32)],
        compiler_params=_params(("arbitrary", "arbitrary")),
        name="peer_ffn",
    )(h_t, u, v_t, rank2, p2, n1, p1, x1)


def _tiles(batch, seq):
    coarsest = DIL_GROUPS[-1][1] * DIL_BLOCK
    assert seq % coarsest == 0, "sequence must hold whole blocks of the most dilated group"
    tokens = batch * seq
    tm = min(512, seq)
    return dict(tm_in=tm, tm_mix=min(512, seq), tq=256, sb_unroll=4, tl=128,
                tt=min(512, tokens), et=2048, ec=256, lt=256)


def _suffix_matrix(n):
    s = np.arange(n)
    same_half = (s[:, None] // LANES) == (s[None, :] // LANES)
    return jnp.asarray(((s[:, None] > s[None, :]) & same_half).astype(np.float32), BF16)


def _block_diag_ones():
    h = np.arange(DIL_W) // HEAD_DIM
    return jnp.asarray((h[:, None] == h[None, :]).astype(np.float32), BF16)


def kernel(x, norm1_gain, w_in, b_gate, q_norm_gain, k_norm_gain, w_sb_out, w_dil_out, w_out,
           norm2_gain, w_peer_q, peer_sub_keys, peer_u, peer_v):
    B, S, D = x.shape
    depth = w_in.shape[0]
    t = _tiles(B, S)
    mj = _suffix_matrix(t["tq"])
    ones_bd = _block_diag_ones()
    x2 = x.reshape(B * S, D)
    for l in range(depth):
        outs = _in_proj(
            x2, norm1_gain[l][None, :], w_in[l].astype(BF16), b_gate[l][None, :],
            q_norm_gain[l].reshape(1, DIL_W), k_norm_gain[l].reshape(1, DIL_W), ones_bd,
            batch=B, seq=S, tm=t["tm_in"])
        q_sb, k_sb, v_sb = outs[0:3]
        q_dl, k_dl, v_dl, gates = outs[3:6], outs[6:9], outs[9:12], outs[12]
        o_sb = _sb_attention(q_sb, k_sb, v_sb, mj, batch=B, seq=S, tq=t["tq"], unroll=t["sb_unroll"])
        o_dl, l_dl = _dil_attention(q_dl, k_dl, v_dl, batch=B, seq=S)
        nhp = 2 * PEER_HEADS
        x1, h_t, s_t = _mix(
            x2, o_sb, gates, o_dl, l_dl, w_sb_out[l].astype(BF16), w_dil_out[l].astype(BF16),
            w_out[l].astype(BF16), norm2_gain[l][None, :], w_peer_q[l].astype(BF16),
            peer_sub_keys[l].reshape(nhp, PEER_N_KEYS, -1).astype(BF16),
            batch=B, seq=S, tm=t["tm_mix"])
        rank2, p2, n1, p1 = _peer_topk(s_t, tl=t["tl"])
        x2 = _peer_ffn(h_t, peer_u[l].astype(BF16), peer_v[l].T.astype(BF16), rank2, p2, n1, p1, x1,
                       tt=t["tt"], et=t["et"], lt=t["lt"], ec=t["ec"])
    return x2.reshape(B, S, D)
```

```python
import functools
import math

import numpy as np
import jax
import jax.numpy as jnp
from jax import lax
from jax.experimental import pallas as pl
from jax.experimental.pallas import tpu as pltpu

F32 = jnp.float32
BF16 = jnp.bfloat16

HEAD_DIM = 64
SB_HEADS = 8
DIL_GROUPS = ((128, 1), (512, 4), (2048, 16))
DIL_HEADS_PER_GROUP = 4
DIL_HEADS = DIL_HEADS_PER_GROUP * len(DIL_GROUPS)
PEER_HEADS = 8
PEER_N_KEYS = 128
PEER_TOPK = 16
RMS_EPS = 1e-6
ALIBI_MAX_EXP = 8.0
SB_W = SB_HEADS * HEAD_DIM
DIL_W = DIL_HEADS * HEAD_DIM
DIL_GROUP_W = DIL_HEADS_PER_GROUP * HEAD_DIM
QK_SCALE = HEAD_DIM ** -0.5
NEG_BIG = -1e30

LANES = 128
DIL_BLOCK = 128
VMEM_LIMIT = 56 * 1024 * 1024


def _slopes():
    h = np.arange(1, DIL_HEADS + 1, dtype=np.float32)
    return np.float32(2.0) ** (np.float32(-ALIBI_MAX_EXP) * h / np.float32(DIL_HEADS))


def _params(sem, flags=None):
    return pltpu.CompilerParams(dimension_semantics=sem, vmem_limit_bytes=VMEM_LIMIT, flags=flags)


def _const_spec(shape):
    return pl.BlockSpec(shape, lambda *_: (0,) * len(shape))


def _in_proj_kernel(x_ref, g1_ref, w_ref, bg_ref, gq_ref, gk_ref, ones_ref,
                    qsb_ref, ksb_ref, vsb_ref,
                    qd0_ref, qd1_ref, qd2_ref, kd0_ref, kd1_ref, kd2_ref,
                    vd0_ref, vd1_ref, vd2_ref, gate_ref, scr_ref, *, tm):
    x = x_ref[...]
    ms = jnp.mean(x * x, axis=-1, keepdims=True)
    xn = (x * lax.rsqrt(ms + RMS_EPS) * g1_ref[...]).astype(BF16)

    def proj(lo, width):
        return jnp.dot(xn, w_ref[:, lo:lo + width], preferred_element_type=F32)

    qsb_ref[...] = (proj(0, SB_W) * (QK_SCALE * math.log2(math.e))).astype(BF16)
    ksb_ref[...] = proj(SB_W, SB_W).astype(BF16)
    vsb_ref[...] = proj(2 * SB_W, SB_W).astype(BF16)

    def head_norm(t, gain):
        ssq = jnp.dot((t * t).astype(BF16), ones_ref[...], preferred_element_type=F32)
        return t * lax.rsqrt(ssq * (1.0 / HEAD_DIM) + RMS_EPS) * gain

    def split_groups(t, outs):
        outs[0][0] = t[:, :DIL_GROUP_W].astype(BF16)
        for g in (1, 2):
            r = DIL_GROUPS[g][1]
            for hf in range(DIL_GROUP_W // LANES):
                lo = g * DIL_GROUP_W + hf * LANES
                scr_ref[hf] = t[:, lo:lo + LANES]
            for c in range(r):
                for hf in range(DIL_GROUP_W // LANES):
                    rows = scr_ref[hf, pl.ds(c, tm // r, stride=r), :]
                    outs[g][0, c, :, hf * LANES:(hf + 1) * LANES] = rows.astype(BF16)

    base = 3 * SB_W
    q = head_norm(proj(base, DIL_W), gq_ref[...]) * QK_SCALE
    split_groups(q, (qd0_ref, qd1_ref, qd2_ref))
    k = head_norm(proj(base + DIL_W, DIL_W), gk_ref[...])
    split_groups(k, (kd0_ref, kd1_ref, kd2_ref))
    split_groups(proj(base + 2 * DIL_W, DIL_W), (vd0_ref, vd1_ref, vd2_ref))

    gw = gate_ref.shape[-1]
    gate_ref[...] = jax.nn.sigmoid(proj(base + 3 * DIL_W, gw) + bg_ref[...]).astype(BF16)


def _in_proj(x2, g1, w_in, b_gate, gq, gk, ones_bd, *, batch, seq, tm):
    T, D = x2.shape
    tiles_per_seq = seq // tm
    gw = b_gate.shape[-1]
    row = lambda i: (i, 0)
    nat = pl.BlockSpec((1, tm, DIL_GROUP_W), lambda i: (i // tiles_per_seq, i % tiles_per_seq, 0))

    def perm_spec(r):
        return pl.BlockSpec((1, r, tm // r, DIL_GROUP_W),
                            lambda i: (i // tiles_per_seq, 0, i % tiles_per_seq, 0))

    def perm_shape(r):
        return jax.ShapeDtypeStruct((batch, r, seq // r, DIL_GROUP_W), BF16)

    r1, r2 = DIL_GROUPS[1][1], DIL_GROUPS[2][1]
    dil_shapes = [jax.ShapeDtypeStruct((batch, seq, DIL_GROUP_W), BF16), perm_shape(r1), perm_shape(r2)]
    dil_specs = [nat, perm_spec(r1), perm_spec(r2)]
    out_shape = ([jax.ShapeDtypeStruct((T, SB_W), BF16)] * 3 + dil_shapes * 3
                 + [jax.ShapeDtypeStruct((T, gw), BF16)])
    out_specs = ([pl.BlockSpec((tm, SB_W), row)] * 3 + dil_specs * 3 + [pl.BlockSpec((tm, gw), row)])
    return pl.pallas_call(
        functools.partial(_in_proj_kernel, tm=tm),
        grid=(T // tm,),
        in_specs=[pl.BlockSpec((tm, D), row), _const_spec(g1.shape), _const_spec(w_in.shape),
                  _const_spec(b_gate.shape), _const_spec(gq.shape), _const_spec(gk.shape),
                  _const_spec(ones_bd.shape)],
        out_specs=out_specs,
        out_shape=out_shape,
        scratch_shapes=[pltpu.VMEM((DIL_GROUP_W // LANES, tm, LANES), F32)],
        compiler_params=_params(("arbitrary",)),
        name="in_proj",
    )(x2, g1, w_in, b_gate, gq, gk, ones_bd)


def _sb_kernel(q_ref, k_ref, v_ref, mj_ref, o_ref, q_scr, c_scr, acc_scr, *, tq, unroll):
    qi = pl.program_id(2)
    lane = lax.broadcasted_iota(jnp.int32, (tq, LANES), 1)
    q = q_ref[0]
    nh = tq // LANES
    sign = jnp.uint32(0x80000000)
    for hh in range(2):
        in_head = (lane >= hh * HEAD_DIM) & (lane < (hh + 1) * HEAD_DIM)
        q_scr[hh] = jnp.where(in_head, q, jnp.zeros_like(q))

    def trip(first_kb, count, diagonal):
        chains = [(hh, u) for u in range(count) for hh in range(2)]
        starts = [pl.multiple_of((first_kb - u) * tq, tq) for u in range(count)]
        carried = [c_scr[0], c_scr[1]]
        accs = [acc_scr[0], acc_scr[1]]
        live = [dict() for _ in chains]

        def qk(i):
            hh, u = chains[i]
            live[i]["z"] = lax.dot_general(q_scr[hh], k_ref[0, pl.ds(starts[u], tq), :],
                                           (((1,), (1,)), ((), ())), preferred_element_type=F32)

        def pointwise(i):
            z = live[i].pop("z")
            neg_abs = lax.bitcast_convert_type(lax.bitcast_convert_type(z, jnp.uint32) | sign, F32)
            t = jnp.maximum(z, 0.0) + jnp.log(1.0 + jnp.exp2(neg_abs)) * math.log2(math.e)
            if diagonal:
                causal = (lax.broadcasted_iota(jnp.int32, (tq, tq), 1)
                          < lax.broadcasted_iota(jnp.int32, (tq, tq), 0))
                t = jnp.where(causal, t, 0.0)
                z = jnp.where(causal, z, NEG_BIG)
            live[i].update(z=z, t=t.astype(BF16),
                           tot=[jnp.sum(t[:, hf * LANES:(hf + 1) * LANES], axis=1, keepdims=True)
                                for hf in range(nh)])

        def suffix(i):
            live[i]["r"] = jnp.dot(live[i].pop("t"), mj_ref[...], preferred_element_type=F32)

        def weigh(i):
            hh, _ = chains[i]
            z, r, tot = live[i].pop("z"), live[i].pop("r"), live[i].pop("tot")
            parts = [None] * nh
            for hf in reversed(range(nh)):
                sl = slice(hf * LANES, (hf + 1) * LANES)
                parts[hf] = jnp.exp2(z[:, sl] - carried[hh] - r[:, sl]).astype(BF16)
                carried[hh] = carried[hh] + tot[hf]
            live[i]["a"] = jnp.concatenate(parts, axis=1)

        def pv(i):
            hh, u = chains[i]
            accs[hh] = accs[hh] + jnp.dot(live[i].pop("a"), v_ref[0, pl.ds(starts[u], tq), :],
                                          preferred_element_type=F32)

        stages = (qk, pointwise, suffix, weigh, pv)
        for step in range(len(chains) + len(stages) - 1):
            for s in reversed(range(len(stages))):
                if 0 <= step - s < len(chains):
                    stages[s](step - s)
        c_scr[0], c_scr[1] = carried
        acc_scr[0], acc_scr[1] = accs

    c_scr[...] = jnp.zeros_like(c_scr)
    acc_scr[...] = jnp.zeros_like(acc_scr)
    trip(qi, 1, True)
    groups = qi // unroll

    def grouped(it, carry):
        trip(qi - 1 - it * unroll, unroll, False)
        return carry

    lax.fori_loop(0, groups, grouped, 0)
    rest = qi - groups * unroll
    size = unroll // 2
    while size >= 1:
        @pl.when((rest & size) != 0)
        def _(size=size):
            trip((rest & (2 * size - 1)) - 1, size, False)
        size //= 2
    o_ref[0] =jnp.where(lane < HEAD_DIM, acc_scr[0], acc_scr[1]).astype(BF16)


def _sb_attention(q, k, v, mj, *, batch, seq, tq, unroll):
    q3, k3, v3 = (t.reshape(batch, seq, SB_W) for t in (q, k, v))
    pairs = SB_W // LANES
    out = pl.pallas_call(
        functools.partial(_sb_kernel, tq=tq, unroll=unroll),
        grid=(batch, pairs, seq // tq),
        in_specs=[pl.BlockSpec((1, tq, LANES), lambda b, p, i: (b, i, p)),
                  pl.BlockSpec((1, seq, LANES), lambda b, p, i: (b, 0, p)),
                  pl.BlockSpec((1, seq, LANES), lambda b, p, i: (b, 0, p)),
                  _const_spec(mj.shape)],
        out_specs=pl.BlockSpec((1, tq, LANES), lambda b, p, i: (b, i, p)),
        out_shape=jax.ShapeDtypeStruct((batch, seq, SB_W), BF16),
        scratch_shapes=[pltpu.VMEM((2, tq, LANES), BF16), pltpu.VMEM((2, tq, LANES), F32),
                        pltpu.VMEM((2, tq, LANES), F32)],
        compiler_params=_params(("arbitrary", "arbitrary", "arbitrary")),
        name="sb_attn",
    )(q3, k3, v3, mj)
    return out.reshape(batch * seq, SB_W)


def _dil_kernel(*refs, seq, slopes):
    n = pl.program_id(1)
    ng = len(DIL_GROUPS)
    q_refs, kc_refs, kp_refs, vc_refs, vp_refs = (refs[i * ng:(i + 1) * ng] for i in range(5))
    o_refs, l_refs = refs[5 * ng:6 * ng], refs[6 * ng:7 * ng]
    tb = DIL_BLOCK
    lane = lax.broadcasted_iota(jnp.int32, (tb, LANES), 1)
    row = lax.broadcasted_iota(jnp.int32, (tb, tb), 0)
    col = lax.broadcasted_iota(jnp.int32, (tb, tb), 1)
    d_cur = row - col
    d_prev = d_cur + tb
    off_cur = d_cur.astype(F32)
    off_prev = d_prev.astype(F32)
    contract = (((1,), (1,)), ((), ()))
    valid_cur = d_cur >= 0
    heads = [(g, jp, hh) for g in range(ng) for jp in range(DIL_GROUP_W // LANES) for hh in range(2)]
    lanes_of = lambda jp: slice(jp * LANES, (jp + 1) * LANES)

    scores = []
    for g, jp, hh in heads:
        sl = lanes_of(jp)
        in_head = (lane >= hh * HEAD_DIM) & (lane < (hh + 1) * HEAD_DIM)
        q = q_refs[g][0, :, sl]
        qm = jnp.where(in_head, q, jnp.zeros_like(q))
        scores.append((lax.dot_general(qm, kc_refs[g][0, :, sl], contract, preferred_element_type=F32),
                       lax.dot_general(qm, kp_refs[g][0, :, sl], contract, preferred_element_type=F32)))

    masked = []
    for (g, jp, hh), (zc, zp) in zip(heads, scores):
        window, dil = DIL_GROUPS[g]
        has_prev = (n % (seq // dil // tb)) != 0
        valid_prev = (d_prev <= window // dil) & has_prev
        slope = float(slopes[g * DIL_HEADS_PER_GROUP + 2 * jp + hh]) * dil
        zc = jnp.where(valid_cur, zc - slope * off_cur, NEG_BIG)
        zp = jnp.where(valid_prev, zp - slope * off_prev, NEG_BIG)
        m = jnp.max(jnp.maximum(zc, zp), axis=1, keepdims=True)
        masked.append((zc, zp, m))

    weighted = []
    for (g, jp, hh), (zc, zp, m) in zip(heads, masked):
        sl = lanes_of(jp)
        pc = jnp.exp(zc - m)
        pp = jnp.exp(zp - m)
        den = jnp.sum(pc + pp, axis=1, keepdims=True)
        acc = (jnp.dot(pc.astype(BF16), vc_refs[g][0, :, sl], preferred_element_type=F32)
               + jnp.dot(pp.astype(BF16), vp_refs[g][0, :, sl], preferred_element_type=F32))
        weighted.append((acc, den, m))

    first = lane < HEAD_DIM
    for i in range(0, len(heads), 2):
        g, jp, _ = heads[i]
        sl = lanes_of(jp)
        (a0, d0, m0), (a1, d1, m1) = weighted[i], weighted[i + 1]
        o_refs[g][0, :, sl] = jnp.where(first, a0 / d0, a1 / d1).astype(BF16)
        l_refs[g][0, :, sl] = jnp.where(first, jnp.broadcast_to(m0 + jnp.log(d0), (tb, LANES)),
                                        jnp.broadcast_to(m1 + jnp.log(d1), (tb, LANES)))


def _dil_attention(qs, ks, vs, *, batch, seq):
    tb = DIL_BLOCK
    flat = lambda t: t.reshape(batch, seq, DIL_GROUP_W)
    qs, ks, vs = ([flat(t) for t in ts] for ts in (qs, ks, vs))
    cur = pl.BlockSpec((1, tb, DIL_GROUP_W), lambda b, n: (b, n, 0))
    prev = pl.BlockSpec((1, tb, DIL_GROUP_W), lambda b, n: (b, jnp.maximum(n - 1, 0), 0))
    ng = len(DIL_GROUPS)
    outs = pl.pallas_call(
        functools.partial(_dil_kernel, seq=seq, slopes=_slopes()),
        grid=(batch, seq // tb),
        in_specs=[cur] * ng + [cur] * ng + [prev] * ng + [cur] * ng + [prev] * ng,
        out_specs=[cur] * (2 * ng),
        out_shape=([jax.ShapeDtypeStruct((batch, seq, DIL_GROUP_W), BF16)] * ng
                   + [jax.ShapeDtypeStruct((batch, seq, DIL_GROUP_W), F32)] * ng),
        compiler_params=_params(("arbitrary", "arbitrary")),
        name="dil_attn",
    )(*qs, *ks, *ks, *vs, *vs)
    return outs[:ng], outs[ng:]


def _mix_kernel(x_ref, osb_ref, gate_ref, o0_ref, o1_ref, o2_ref, l0_ref, l1_ref, l2_ref,
                wsb_ref, wdl_ref, wout_ref, g2_ref, wq_ref, sk_ref,
                x1_ref, ht_ref, st_ref, o_scr, l_scr, *, tm):
    def natural(o_ref, l_ref, g):
        r = DIL_GROUPS[g][1]
        if r == 1:
            return o_ref[0].astype(F32), l_ref[0]
        nh = DIL_GROUP_W // LANES
        for c in range(r):
            for hf in range(nh):
                sl = slice(hf * LANES, (hf + 1) * LANES)
                o_scr[hf, pl.ds(c, tm // r, stride=r), :] = o_ref[0, c, :, sl].astype(F32)
                l_scr[hf, pl.ds(c, tm // r, stride=r), :] = l_ref[0, c, :, sl]
        return (jnp.concatenate([o_scr[hf] for hf in range(nh)], axis=1),
                jnp.concatenate([l_scr[hf] for hf in range(nh)], axis=1))

    o0, l0 = natural(o0_ref, l0_ref, 0)
    o1, l1 = natural(o1_ref, l1_ref, 1)
    o2, l2 = natural(o2_ref, l2_ref, 2)
    mx = jnp.maximum(jnp.maximum(l0, l1), l2)
    w0, w1, w2 = jnp.exp(l0 - mx), jnp.exp(l1 - mx), jnp.exp(l2 - mx)
    o_dl = (w0 * o0 + w1 * o1 + w2 * o2) / (w0 + w1 + w2)

    d = x_ref.shape[-1]
    y_sb = jnp.dot(osb_ref[...], wsb_ref[...], preferred_element_type=F32)
    y_dl = jnp.dot(o_dl.astype(BF16), wdl_ref[...], preferred_element_type=F32)
    mixed = gate_ref[:, :d].astype(F32) * y_sb + gate_ref[:, d:].astype(F32) * y_dl
    x1 = x_ref[...] + jnp.dot(mixed.astype(BF16), wout_ref[...], preferred_element_type=F32)
    x1_ref[...] = x1

    ms = jnp.mean(x1 * x1, axis=-1, keepdims=True)
    h = x1 * lax.rsqrt(ms + RMS_EPS) * g2_ref[...]
    ht_ref[...] = h.T.astype(BF16)
    qp = jnp.dot(h.astype(BF16), wq_ref[...], preferred_element_type=F32).astype(BF16)
    for hp in range(2 * PEER_HEADS):
        st_ref[hp] = lax.dot_general(sk_ref[hp], qp[:, hp * PEER_N_KEYS:(hp + 1) * PEER_N_KEYS],
                                     (((1,), (1,)), ((), ())), preferred_element_type=F32)


def _mix(x2, o_sb, gates, o_dl, l_dl, w_sb, w_dl, w_out, g2, w_q, sub_keys, *, batch, seq, tm):
    T, D = x2.shape
    tiles_per_seq = seq // tm
    row = lambda i: (i, 0)

    def group_spec(r):
        if r == 1:
            return pl.BlockSpec((1, tm, DIL_GROUP_W), lambda i: (i // tiles_per_seq, i % tiles_per_seq, 0))
        return pl.BlockSpec((1, r, tm // r, DIL_GROUP_W),
                            lambda i: (i // tiles_per_seq, 0, i % tiles_per_seq, 0))

    def group_view(t, r):
        return t if r == 1 else t.reshape(batch, r, seq // r, DIL_GROUP_W)

    rs = [dil for _, dil in DIL_GROUPS]
    o_dl = [group_view(t, r) for t, r in zip(o_dl, rs)]
    l_dl = [group_view(t, r) for t, r in zip(l_dl, rs)]
    g_specs = [group_spec(r) for r in rs]
    nhp = 2 * PEER_HEADS
    return pl.pallas_call(
        functools.partial(_mix_kernel, tm=tm),
        grid=(T // tm,),
        in_specs=[pl.BlockSpec((tm, D), row), pl.BlockSpec((tm, SB_W), row),
                  pl.BlockSpec((tm, gates.shape[-1]), row)] + g_specs + g_specs
                 + [_const_spec(w.shape) for w in (w_sb, w_dl, w_out, g2, w_q, sub_keys)],
        out_specs=[pl.BlockSpec((tm, D), row), pl.BlockSpec((D, tm), lambda i: (0, i)),
                   pl.BlockSpec((nhp, PEER_N_KEYS, tm), lambda i: (0, 0, i))],
        out_shape=[jax.ShapeDtypeStruct((T, D), F32), jax.ShapeDtypeStruct((D, T), BF16),
                   jax.ShapeDtypeStruct((nhp, PEER_N_KEYS, T), F32)],
        scratch_shapes=[pltpu.VMEM((DIL_GROUP_W // LANES, tm, LANES), F32)] * 2,
        compiler_params=_params(("arbitrary",)),
        name="mix",
    )(x2, o_sb, gates, *o_dl, *l_dl, w_sb, w_dl, w_out, g2, w_q, sub_keys)


def _staircase():
    K = PEER_TOPK
    rows = [(0, b) for b in range(K)]
    for a in range(1, 8):
        rows += [(a, b) for b in range(8)]
    rows += [(a, 0) for a in range(8, K)]
    a = np.array([r[0] for r in rows])
    b = np.array([r[1] for r in rows])
    return a, b, (a + 1) * (b + 1) <= K


def _topk_kernel(st_ref, flat_ref, rank2_ref, p2_ref, n1_ref, p1_ref, tie_scr, *, tl):
    K = PEER_TOPK
    nk = PEER_N_KEYS
    key_id = lax.broadcasted_iota(jnp.int32, (nk, tl), 0).astype(F32)
    flat = flat_ref[...]
    ok = flat >= 0.0
    neg_inf = -jnp.inf

    def removed(s, valid=None):
        gone = s == neg_inf if valid is None else (s == neg_inf) & valid
        return jnp.sum(jnp.where(gone, 1.0, 0.0), axis=0, keepdims=True)

    def top16_fast(s1, s2):
        state = [[s1, None, []], [s2, jnp.full((nk, tl), float(K), F32), []]]
        for k in range(K):
            for st in state:
                m = jnp.max(st[0], axis=0, keepdims=True)
                hit = st[0] == m
                if st[1] is not None:
                    st[1] = jnp.where(hit, float(k), st[1])
                st[0] = jnp.where(hit, neg_inf, st[0])
                st[2].append(m)
        return [(vals, rank, removed(s) != float(K)) for s, rank, vals in state]

    def top16_exact(s1, s2):
        out = []
        for s in (s1, s2):
            rank = jnp.full((nk, tl), float(K), F32)
            vals = []
            for k in range(K):
                m = jnp.max(s, axis=0, keepdims=True)
                idx = jnp.min(jnp.where(s == m, key_id, float(nk)), axis=0, keepdims=True)
                hit = key_id == idx
                rank = jnp.where(hit, float(k), rank)
                s = jnp.where(hit, neg_inf, s)
                vals.append(m)
            out.append((vals, rank, None))
        return out

    def candidates(v1, v2):
        v2lo = jnp.concatenate(v2[:8], axis=0)
        v2hi = jnp.concatenate(v2[8:], axis=0)
        groups = [v1[0] + v2lo, v1[0] + v2hi] + [v1[a] + v2lo for a in range(1, 8)]
        groups.append(jnp.concatenate(v1[8:], axis=0) + v2[0])
        return jnp.where(ok, jnp.concatenate(groups, axis=0), neg_inf)

    def counts_fast(cand, best_max, rank1, s1, v1):
        zsum = jnp.zeros((1, tl), F32)
        for k in range(K):
            m = jnp.max(cand, axis=0, keepdims=True)
            cand = jnp.where(cand == m, neg_inf, cand)
            zsum = zsum + jnp.exp(m - best_max)
        taken = jnp.where((cand == neg_inf) & ok, 1.0, 0.0)
        per_a = [jnp.sum(taken[0:16], axis=0, keepdims=True)]
        per_a += [jnp.sum(taken[8 + 8 * a:16 + 8 * a], axis=0, keepdims=True) for a in range(1, 8)]
        per_a += [taken[72 + j:73 + j] for j in range(8)]
        n1 = jnp.zeros((nk, tl), F32)
        for a in range(K):
            n1 = jnp.where(s1 == v1[a], per_a[a], n1)
        tie = jnp.sum(taken, axis=0, keepdims=True) != float(K)
        return n1, zsum, tie

    def counts_exact(cand, best_max, rank1, s1, v1):
        zsum = jnp.zeros((1, tl), F32)
        n1 = jnp.zeros((nk, tl), F32)
        for k in range(K):
            m = jnp.max(cand, axis=0, keepdims=True)
            f = jnp.min(jnp.where(cand == m, flat, float(K * K)), axis=0, keepdims=True)
            cand = jnp.where(flat == f, neg_inf, cand)
            zsum = zsum + jnp.exp(m - best_max)
            n1 = n1 + jnp.where(rank1 == jnp.floor(f * (1.0 / K)), 1.0, 0.0)
        return n1, zsum, None

    def head(h, top16, counts):
        s1 = st_ref[2 * h]
        s2 = st_ref[2 * h + 1]
        (v1, rank1, tie1), (v2, rank2, tie2) = top16(s1, s2)
        best_max = v1[0] + v2[0]
        n1, zsum, tie3 = counts(candidates(v1, v2), best_max, rank1, s1, v1)
        rank2_ref[h] = rank2.astype(BF16)
        p2_ref[h] = jnp.exp(s2 - v2[0]).astype(BF16)
        n1_ref[h] = n1
        p1_ref[h] = 0.5 * jnp.exp(s1 - v1[0]) / zsum
        if tie1 is not None:
            tie_scr[h:h + 1, :] = jnp.where(tie1 | tie2 | tie3, 1.0, 0.0)

    for h in range(PEER_HEADS):
        head(h, top16_fast, counts_fast)

    @pl.when(jnp.max(tie_scr[...]) > 0.0)
    def _():
        for h in range(PEER_HEADS):
            @pl.when(jnp.max(tie_scr[h:h + 1, :]) > 0.0)
            def _(h=h):
                head(h, top16_exact, counts_exact)


def _peer_topk(s_t, *, tl):
    nhp, nk, T = s_t.shape
    a, b, okr = _staircase()
    flat = jnp.asarray(np.where(okr, a * PEER_TOPK + b, -1).astype(np.float32)[:, None])
    spec = pl.BlockSpec((PEER_HEADS, nk, tl), lambda i: (0, 0, i))
    return pl.pallas_call(
        functools.partial(_topk_kernel, tl=tl),
        grid=(T // tl,),
        in_specs=[pl.BlockSpec((nhp, nk, tl), lambda i: (0, 0, i)), _const_spec(flat.shape)],
        out_specs=[spec] * 4,
        out_shape=[jax.ShapeDtypeStruct((PEER_HEADS, nk, T), BF16)] * 2
                  + [jax.ShapeDtypeStruct((PEER_HEADS, nk, T), F32)] * 2,
        scratch_shapes=[pltpu.VMEM((PEER_HEADS, tl), F32)],
        compiler_params=_params(("arbitrary",)),
        name="peer_topk",
    )(s_t, flat)


def _ffn_kernel(ht_ref, u_ref, vt_ref, rank2_ref, p2_ref, n1_ref, p1_ref, x1_ref,
                out_ref, a_scr, h_scr, acc_scr, *, et, tt, lt, ec):
    e = pl.program_id(1)
    nk = PEER_N_KEYS
    rows = 16

    @pl.when(e == 0)
    def _():
        acc_scr[...] = jnp.zeros_like(acc_scr)

    def packed_row(ref, h, i1_local, ts):
        group = pl.multiple_of(e * (et // nk) + (i1_local // 8) * 8, 8)
        values = ref[h, pl.ds(group, 8), ts]
        tile = jnp.broadcast_to(values[i1_local % 8:i1_local % 8 + 1, :], (rows, lt)).astype(BF16)
        return jnp.concatenate([tile] * (nk // rows), axis=0)

    ahead = a_scr.shape[0] - 1

    def expert_scores(c):
        if c < et // ec:
            a_scr[c % (ahead + 1)] = jnp.dot(u_ref[c * ec:(c + 1) * ec, :], ht_ref[...],
                                             preferred_element_type=F32)

    for c in range(ahead):
        expert_scores(c)
    group = h_scr.shape[1] // ec
    for c in range(et // ec):
        grp = c // group
        expert_scores(c + ahead)
        for j in range(ec // nk):
            i1 = c * (ec // nk) + j
            rs = slice(j * nk, (j + 1) * nk)
            for t0 in range(0, tt, lt):
                ts = slice(t0, t0 + lt)
                half_gate = jnp.zeros((nk, lt), BF16)
                for h in range(PEER_HEADS):
                    cnt = packed_row(n1_ref, h, i1, ts)
                    p1 = packed_row(p1_ref, h, i1, ts)
                    sel = jnp.where(rank2_ref[h, :, ts] < cnt, p2_ref[h, :, ts], jnp.zeros((nk, lt), BF16))
                    half_gate = half_gate + sel * p1
                a = a_scr[c % (ahead + 1), rs, ts]
                twice_gelu = a + a * lax.erf(a * math.sqrt(0.5))
                h_scr[grp % 2, (c % group) * ec + j * nk:(c % group) * ec + (j + 1) * nk, ts] = (
                    twice_gelu.astype(BF16) * half_gate)
        if (c + 1) % group == 0:
            acc_scr[...] += jnp.dot(vt_ref[:, grp * group * ec:(grp + 1) * group * ec], h_scr[grp % 2],
                                    preferred_element_type=F32)

    @pl.when(e == pl.num_programs(1) - 1)
    def _():
        out_ref[...] = x1_ref[...] + acc_scr[...].T


def _peer_ffn(h_t, u, v_t, rank2, p2, n1, p1, x1, *, tt, et, lt, ec):
    D, T = h_t.shape
    E = u.shape[0]
    nk = PEER_N_KEYS
    tab = pl.BlockSpec((PEER_HEADS, nk, tt), lambda t, e: (0, 0, t))
    return pl.pallas_call(
        functools.partial(_ffn_kernel, et=et, tt=tt, lt=lt, ec=ec),
        grid=(T // tt, E // et),
        in_specs=[pl.BlockSpec((D, tt), lambda t, e: (0, t)),
                  pl.BlockSpec((et, D), lambda t, e: (e, 0)),
                  pl.BlockSpec((D, et), lambda t, e: (0, e)),
                  tab, tab, tab, tab,
                  pl.BlockSpec((tt, D), lambda t, e: (t, 0))],
        out_specs=pl.BlockSpec((tt, D), lambda t, e: (t, 0)),
        out_shape=jax.ShapeDtypeStruct((T, D), F32),
        scratch_shapes=[pltpu.VMEM((et // ec, ec, tt), F32), pltpu.VMEM((2, 2 * ec, tt), BF16),
                        pltpu.VMEM((D, tt), F32)],
        compiler_params=_params(("arbitrary", "arbitrary")),
        name="peer_ffn",
    )(h_t, u, v_t, rank2, p2, n1, p1, x1)


def _tiles(batch, seq):
    coarsest = DIL_GROUPS[-1][1] * DIL_BLOCK
    assert seq % coarsest == 0, "sequence must hold whole blocks of the most dilated group"
    tokens = batch * seq
    tm = min(512, seq)
    return dict(tm_in=tm, tm_mix=min(512, seq), tq=256, sb_unroll=4, tl=128,
                tt=min(512, tokens), et=2048, ec=256, lt=256)


def _suffix_matrix(n):
    s = np.arange(n)
    same_half = (s[:, None] // LANES) == (s[None, :] // LANES)
    return jnp.asarray(((s[:, None] >= s[None, :]) & same_half).astype(np.float32), BF16)


def _block_diag_ones():
    h = np.arange(DIL_W) // HEAD_DIM
    return jnp.asarray((h[:, None] == h[None, :]).astype(np.float32), BF16)


def kernel(x, norm1_gain, w_in, b_gate, q_norm_gain, k_norm_gain, w_sb_out, w_dil_out, w_out,
           norm2_gain, w_peer_q, peer_sub_keys, peer_u, peer_v):
    B, S, D = x.shape
    depth = w_in.shape[0]
    t = _tiles(B, S)
    mj = _suffix_matrix(t["tq"])
    ones_bd = _block_diag_ones()
    x2 = x.reshape(B * S, D)
    for l in range(depth):
        outs = _in_proj(
            x2, norm1_gain[l][None, :], w_in[l].astype(BF16), b_gate[l][None, :],
            q_norm_gain[l].reshape(1, DIL_W), k_norm_gain[l].reshape(1, DIL_W), ones_bd,
            batch=B, seq=S, tm=t["tm_in"])
        q_sb, k_sb, v_sb = outs[0:3]
        q_dl, k_dl, v_dl, gates = outs[3:6], outs[6:9], outs[9:12], outs[12]
        o_sb = _sb_attention(q_sb, k_sb, v_sb, mj, batch=B, seq=S, tq=t["tq"], unroll=t["sb_unroll"])
        o_dl, l_dl = _dil_attention(q_dl, k_dl, v_dl, batch=B, seq=S)
        nhp = 2 * PEER_HEADS
        x1, h_t, s_t = _mix(
            x2, o_sb, gates, o_dl, l_dl, w_sb_out[l].astype(BF16), w_dil_out[l].astype(BF16),
            w_out[l].astype(BF16), norm2_gain[l][None, :], w_peer_q[l].astype(BF16),
            peer_sub_keys[l].reshape(nhp, PEER_N_KEYS, -1).astype(BF16),
            batch=B, seq=S, tm=t["tm_mix"])
        rank2, p2, n1, p1 = _peer_topk(s_t, tl=t["tl"])
        x2 = _peer_ffn(h_t, peer_u[l].astype(BF16), peer_v[l].T.astype(BF16), rank2, p2, n1, p1, x1,
                       tt=t["tt"], et=t["et"], lt=t["lt"], ec=t["ec"])
    return x2.reshape(B, S, D)
```
